```python
import jax, jax.numpy as jnp
from jax import lax
import numpy as np

D_MODEL = 4096
BATCH = 4
SEQ = 4096
DEPTH = 2

CHUNK = 64
Q_BLOCK = 128
N_A_LAYERS = DEPTH // 2
N_B_LAYERS = DEPTH - N_A_LAYERS
N_DENSE_LAYERS = (DEPTH + 1) // 2
N_MOE_LAYERS = DEPTH // 2
D_RNN = D_MODEL
LRU_BLOCKS = 16
LRU_BLOCK = D_RNN // LRU_BLOCKS
CONV_WIDTH = 4
LRU_C = 8.0
N_HEADS = D_MODEL // 128
QK_NOPE = 128
QK_ROPE = 64
V_DIM = 128
Q_RANK = D_MODEL // 4
KV_RANK = D_MODEL // 8
ROPE_THETA = 10000.0
ATTN_SCALE = (QK_NOPE + QK_ROPE) ** -0.5
D_FF_DENSE = 3 * D_MODEL
N_EXPERTS = 8
TOP_K = 2
D_FF_EXPERT = D_FF_DENSE // TOP_K
MOE_BLOCK = 512
EPS = 1e-6

kernel_name = 'yoco_rglru_mla_moe_streaming_block'


def rms_norm(x, g):
    xf = x.astype(jnp.float32)
    y = xf * lax.rsqrt(jnp.mean(xf * xf, axis=-1, keepdims=True) + EPS) * g.astype(jnp.float32)
    return y.astype(x.dtype)


def rope_tables(s):
    inv_freq = ROPE_THETA ** (-jnp.arange(0, QK_ROPE, 2, dtype=jnp.float32) / QK_ROPE)
    ang = jnp.arange(s, dtype=jnp.float32)[:, None] * inv_freq[None, :]
    return jnp.cos(ang), jnp.sin(ang)


def apply_rope(x, cos, sin):
    xf = x.astype(jnp.float32)
    x1, x2 = jnp.split(xf, 2, axis=-1)
    return jnp.concatenate([x1 * cos - x2 * sin, x1 * sin + x2 * cos], axis=-1).astype(x.dtype)


def rglru_mixer(xn, w_in, conv_w, conv_b, w_rgate, b_rgate, w_igate, b_igate, lru_lambda, w_out):
    b, s, _ = xn.shape
    gate_in, rec_in = jnp.split(xn @ w_in, 2, axis=-1)
    u = lax.conv_general_dilated(rec_in, conv_w[:, None, :], window_strides=(1,),
                                 padding=[(CONV_WIDTH - 1, 0)],
                                 dimension_numbers=('NWC', 'WIO', 'NWC'),
                                 feature_group_count=D_RNN) + conv_b
    ub = u.reshape(b, s, LRU_BLOCKS, LRU_BLOCK)
    r = jax.nn.sigmoid(jnp.einsum('bsgi,gij->bsgj', ub, w_rgate).reshape(b, s, D_RNN) + b_rgate)
    i = jax.nn.sigmoid(jnp.einsum('bsgi,gij->bsgj', ub, w_igate).reshape(b, s, D_RNN) + b_igate)
    log_a = -LRU_C * r.astype(jnp.float32) * jax.nn.softplus(-lru_lambda.astype(jnp.float32))
    a = jnp.exp(log_a)
    xin = (i * u).astype(jnp.float32) * jnp.sqrt(-jnp.expm1(2.0 * log_a))

    def step(h, inp):
        a_t, x_t = inp
        h = a_t * h + x_t
        return h, h

    _, hs = lax.scan(step, jnp.zeros((b, D_RNN), jnp.float32),
                     (jnp.swapaxes(a, 0, 1), jnp.swapaxes(xin, 0, 1)))
    h = jnp.swapaxes(hs, 0, 1).astype(xn.dtype)
    return (jax.nn.gelu(gate_in, approximate=True) * h) @ w_out


def shared_latent_kv(x, g_kv, w_kv_down, g_kv_latent, w_kv_rope, w_up_k, w_up_v, cos, sin):
    h = rms_norm(x, g_kv)
    c_kv = rms_norm(h @ w_kv_down, g_kv_latent)
    k_rope = apply_rope(h @ w_kv_rope, cos, sin)
    k_nope = jnp.einsum('bsc,chd->bshd', c_kv, w_up_k)
    v = jnp.einsum('bsc,chd->bshd', c_kv, w_up_v)
    return k_nope, k_rope, v


def mla_mixer(xn, w_q_down, g_q_latent, w_q_up, w_q_rope, w_out, k_nope, k_rope, v, cos, sin):
    b, s, _ = xn.shape
    c_q = rms_norm(xn @ w_q_down, g_q_latent)
    q_nope = jnp.einsum('bsr,rhd->bshd', c_q, w_q_up)
    q_rope = apply_rope(jnp.einsum('bsr,rhd->bshd', c_q, w_q_rope), cos[:, None, :], sin[:, None, :])
    chunk_id = jnp.arange(s) // CHUNK
    outs = []
    for q0 in range(0, s, Q_BLOCK):
        q1 = q0 + Q_BLOCK
        scores = (jnp.einsum('bqhd,bkhd->bhqk', q_nope[:, q0:q1], k_nope[:, :q1],
                             preferred_element_type=jnp.float32)
                  + jnp.einsum('bqhr,bkr->bhqk', q_rope[:, q0:q1], k_rope[:, :q1],
                               preferred_element_type=jnp.float32)) * ATTN_SCALE
        visible = chunk_id[q0:q1, None] >= chunk_id[None, :q1]
        probs = jax.nn.softmax(jnp.where(visible, scores, -jnp.inf), axis=-1)
        outs.append(jnp.einsum('bhqk,bkhd->bqhd', probs.astype(v.dtype), v[:, :q1]))
    o = jnp.concatenate(outs, axis=1).reshape(b, s, N_HEADS * V_DIM)
    return o @ w_out


def swiglu(x, w_gate, w_up, w_down):
    return (jax.nn.silu(x @ w_gate) * (x @ w_up)) @ w_down


def moe_swiglu(x, w_router, w_gate, w_up, w_down):
    b, s, d = x.shape
    n = b * s
    xt = x.reshape(n, d)
    logits = jnp.einsum('nd,de->ne', xt, w_router, preferred_element_type=jnp.float32)
    top_logit, top_idx = lax.top_k(logits, TOP_K)
    gates = jax.nn.softmax(top_logit, axis=-1)
    flat_e = top_idx.reshape(-1)
    flat_tok = jnp.arange(n * TOP_K, dtype=jnp.int32) // TOP_K
    order = jnp.argsort(flat_e)
    sorted_e = flat_e[order]
    counts = jnp.bincount(flat_e, length=N_EXPERTS)
    padded = (counts + MOE_BLOCK - 1) // MOE_BLOCK * MOE_BLOCK
    start = jnp.cumsum(counts) - counts
    pend = jnp.cumsum(padded)
    pstart = pend - padded
    dest = pstart[sorted_e] + jnp.arange(n * TOP_K, dtype=jnp.int32) - start[sorted_e]
    n_rows = -(-(n * TOP_K + N_EXPERTS * (MOE_BLOCK - 1)) // MOE_BLOCK) * MOE_BLOCK
    n_blocks = n_rows // MOE_BLOCK
    row_tok = jnp.full((n_rows,), n, jnp.int32).at[dest].set(flat_tok[order])
    row_gate = jnp.zeros((n_rows,), jnp.float32).at[dest].set(gates.reshape(-1)[order])
    block_e = jnp.minimum(jnp.searchsorted(pend, jnp.arange(n_blocks) * MOE_BLOCK, side='right'),
                          N_EXPERTS - 1)
    x_rows = jnp.concatenate([xt, jnp.zeros((1, d), xt.dtype)], axis=0)[row_tok]
    x_rows = x_rows.reshape(n_blocks, MOE_BLOCK, d)

    def expert_block(args):
        xb, e = args
        hb = jax.nn.silu(xb @ w_gate[e]) * (xb @ w_up[e])
        return hb @ w_down[e]

    y_rows = lax.map(expert_block, (x_rows, block_e)).reshape(n_rows, d)
    y = jnp.zeros((n, d), jnp.float32).at[row_tok].add(
        y_rows.astype(jnp.float32) * row_gate[:, None], mode='drop')
    return y.astype(x.dtype).reshape(b, s, d)


def setup_inputs(seed: int = 0) -> dict:
    key = jax.random.key(seed)
    ks = iter(jax.random.split(key, 48))
    f32 = jnp.float32

    def dense(shape, fan_in):
        return jax.random.normal(next(ks), shape, f32) * (fan_in ** -0.5)

    def gain(shape):
        return 1.0 + 0.02 * jax.random.normal(next(ks), shape, f32)

    def bias(shape):
        return 0.01 * jax.random.normal(next(ks), shape, f32)

    na, nb, nd, nm = N_A_LAYERS, N_B_LAYERS, N_DENSE_LAYERS, N_MOE_LAYERS
    x = jax.random.normal(next(ks), (BATCH, SEQ, D_MODEL), f32)
    a0 = jax.random.uniform(next(ks), (na, D_RNN), f32, 0.9, 0.999)
    lru_lambda = jnp.log(a0) - jnp.log1p(-a0)
    return {
        'x': x,
        'a_norm': gain((na, D_MODEL)),
        'a_w_in': dense((na, D_MODEL, 2 * D_RNN), D_MODEL),
        'a_conv_w': dense((na, CONV_WIDTH, D_RNN), CONV_WIDTH),
        'a_conv_b': bias((na, D_RNN)),
        'a_w_rgate': dense((na, LRU_BLOCKS, LRU_BLOCK, LRU_BLOCK), LRU_BLOCK),
        'a_b_rgate': bias((na, D_RNN)),
        'a_w_igate': dense((na, LRU_BLOCKS, LRU_BLOCK, LRU_BLOCK), LRU_BLOCK),
        'a_b_igate': bias((na, D_RNN)),
        'a_lambda': lru_lambda,
        'a_w_out': dense((na, D_RNN, D_MODEL), D_RNN),
        'kv_norm': gain((D_MODEL,)),
        'kv_w_down': dense((D_MODEL, KV_RANK), D_MODEL),
        'kv_latent_norm': gain((KV_RANK,)),
        'kv_w_rope': dense((D_MODEL, QK_ROPE), D_MODEL),
        'kv_w_up_k': dense((KV_RANK, N_HEADS, QK_NOPE), KV_RANK),
        'kv_w_up_v': dense((KV_RANK, N_HEADS, V_DIM), KV_RANK),
        'b_norm': gain((nb, D_MODEL)),
        'b_w_q_down': dense((nb, D_MODEL, Q_RANK), D_MODEL),
        'b_q_latent_norm': gain((nb, Q_RANK)),
        'b_w_q_up': dense((nb, Q_RANK, N_HEADS, QK_NOPE), Q_RANK),
        'b_w_q_rope': dense((nb, Q_RANK, N_HEADS, QK_ROPE), Q_RANK),
        'b_w_out': dense((nb, N_HEADS * V_DIM, D_MODEL), N_HEADS * V_DIM),
        'ffn_norm': gain((nd, D_MODEL)),
        'ffn_w_gate': dense((nd, D_MODEL, D_FF_DENSE), D_MODEL),
        'ffn_w_up': dense((nd, D_MODEL, D_FF_DENSE), D_MODEL),
        'ffn_w_down': dense((nd, D_FF_DENSE, D_MODEL), D_FF_DENSE),
        'moe_norm': gain((nm, D_MODEL)),
        'moe_w_router': dense((nm, D_MODEL, N_EXPERTS), D_MODEL),
        'moe_w_gate': dense((nm, N_EXPERTS, D_MODEL, D_FF_EXPERT), D_MODEL),
        'moe_w_up': dense((nm, N_EXPERTS, D_MODEL, D_FF_EXPERT), D_MODEL),
        'moe_w_down': dense((nm, N_EXPERTS, D_FF_EXPERT, D_MODEL), D_FF_EXPERT),
        'final_norm': gain((D_MODEL,)),
    }


def reference(x, a_norm, a_w_in, a_conv_w, a_conv_b, a_w_rgate, a_b_rgate, a_w_igate, a_b_igate,
              a_lambda, a_w_out, kv_norm, kv_w_down, kv_latent_norm, kv_w_rope, kv_w_up_k, kv_w_up_v,
              b_norm, b_w_q_down, b_q_latent_norm, b_w_q_up, b_w_q_rope, b_w_out,
              ffn_norm, ffn_w_gate, ffn_w_up, ffn_w_down,
              moe_norm, moe_w_router, moe_w_gate, moe_w_up, moe_w_down, final_norm):
    cos, sin = rope_tables(x.shape[1])
    k_nope = k_rope = v = None
    for layer in range(DEPTH):
        if layer < N_A_LAYERS:
            i = layer
            x = x + rglru_mixer(rms_norm(x, a_norm[i]), a_w_in[i], a_conv_w[i], a_conv_b[i],
                                a_w_rgate[i], a_b_rgate[i], a_w_igate[i], a_b_igate[i],
                                a_lambda[i], a_w_out[i])
        else:
            if layer == N_A_LAYERS:
                k_nope, k_rope, v = shared_latent_kv(x, kv_norm, kv_w_down, kv_latent_norm,
                                                     kv_w_rope, kv_w_up_k, kv_w_up_v, cos, sin)
            j = layer - N_A_LAYERS
            x = x + mla_mixer(rms_norm(x, b_norm[j]), b_w_q_down[j], b_q_latent_norm[j],
                              b_w_q_up[j], b_w_q_rope[j], b_w_out[j], k_nope, k_rope, v, cos, sin)
        f = layer // 2
        if layer % 2 == 0:
            x = x + swiglu(rms_norm(x, ffn_norm[f]), ffn_w_gate[f], ffn_w_up[f], ffn_w_down[f])
        else:
            x = x + moe_swiglu(rms_norm(x, moe_norm[f]), moe_w_router[f], moe_w_gate[f],
                               moe_w_up[f], moe_w_down[f])
    return rms_norm(x, final_norm)
```

```python
import functools

import jax
import jax.numpy as jnp
from jax import lax
from jax.experimental import pallas as pl
from jax.experimental.pallas import tpu as pltpu

F32 = jnp.float32
BF16 = jnp.bfloat16

CHUNK = 64
LRU_BLOCK = 256
CONV_WIDTH = 4
LRU_C = 8.0
QK_NOPE = 128
QK_ROPE = 64
V_DIM = 128
ROPE_THETA = 10000.0
ATTN_SCALE = (QK_NOPE + QK_ROPE) ** -0.5
TOP_K = 2
EPS = 1e-6

VMEM_LIMIT_BYTES = 56 * 1024 * 1024
LANES = 128
SUBLANES = 8
Q_HEAD_STRIDE = 2 * LANES
NEG_BIG = -1e30

MOE_BLOCK = 512


def _cparams(*sem):
    return pltpu.CompilerParams(dimension_semantics=sem, vmem_limit_bytes=VMEM_LIMIT_BYTES)


def _rms(x, g):
    return x * lax.rsqrt(jnp.mean(x * x, axis=-1, keepdims=True) + EPS) * g


def _norm_kernel(n_out, x_ref, *refs):
    g_refs, o_refs = refs[:n_out], refs[n_out:]
    x = x_ref[...]
    y = x * lax.rsqrt(jnp.mean(x * x, axis=-1, keepdims=True) + EPS)
    for g_ref, o_ref in zip(g_refs, o_refs):
        o_ref[...] = (y * g_ref[...]).astype(o_ref.dtype)


def _norm(x, gains, out_dtype=BF16, tm=256):
    m, d = x.shape
    tm = min(tm, m)
    n = len(gains)
    row = pl.BlockSpec((tm, d), lambda i: (i, 0))
    vec = pl.BlockSpec((1, d), lambda i: (0, 0))
    outs = pl.pallas_call(
        functools.partial(_norm_kernel, n),
        out_shape=[jax.ShapeDtypeStruct((m, d), out_dtype)] * n,
        grid=(m // tm,),
        in_specs=[row] + [vec] * n,
        out_specs=[row] * n,
        compiler_params=_cparams("parallel"),
        name="rmsnorm",
    )(x, *[g.reshape(1, d) for g in gains])
    return outs


def _mm_kernel(nw, ne, no, nk, epilogue, a_ref, *refs):
    w_refs = refs[:nw]
    e_refs = refs[nw:nw + ne]
    o_refs = refs[nw + ne:nw + ne + no]
    acc_refs = refs[nw + ne + no:]

    def finish(accs):
        vals = epilogue(accs, e_refs)
        for o_ref, v in zip(o_refs, vals):
            o_ref[...] = v.astype(o_ref.dtype)

    if nk == 1:
        finish([jnp.dot(a_ref[...], w[...], preferred_element_type=F32) for w in w_refs])
        return

    k = pl.program_id(2)

    @pl.when(k == 0)
    def _():
        for acc in acc_refs:
            acc[...] = jnp.zeros_like(acc)

    for acc, w in zip(acc_refs, w_refs):
        acc[...] += jnp.dot(a_ref[...], w[...], preferred_element_type=F32)

    @pl.when(k == nk - 1)
    def _():
        finish([acc[...] for acc in acc_refs])


def _matmul(a, ws, tns, epilogue, extras, outs, *, tm, tk=None, name):
    m, kdim = a.shape
    tm = min(tm, m)
    tk = kdim if tk is None else min(tk, kdim)
    nk = kdim // tk
    nj = ws[0].shape[1] // tns[0]
    in_specs = [pl.BlockSpec((tm, tk), lambda i, j, k: (i, k))]
    for w, tn in zip(ws, tns):
        assert w.shape[1] // tn == nj
        in_specs.append(pl.BlockSpec((tk, tn), lambda i, j, k: (k, j)))
    for _, bs, im in extras:
        in_specs.append(pl.BlockSpec(bs, functools.partial(lambda im, i, j, k: im(i, j), im)))
    out_shape = [jax.ShapeDtypeStruct((m, n), dt) for n, dt, _ in outs]
    out_specs = [pl.BlockSpec((tm, tn), lambda i, j, k: (i, j)) for _, _, tn in outs]
    scratch = [pltpu.VMEM((tm, tn), F32) for tn in tns] if nk > 1 else []
    res = pl.pallas_call(
        functools.partial(_mm_kernel, len(ws), len(extras), len(outs), nk, epilogue),
        out_shape=out_shape,
        grid=(m // tm, nj, nk),
        in_specs=in_specs,
        out_specs=out_specs,
        scratch_shapes=scratch,
        compiler_params=_cparams("parallel", "parallel", "arbitrary"),
        name=name,
    )(a, *ws, *[e[0] for e in extras])
    return res


def _ep_identity(accs, e_refs):
    return accs


def _ep_residual(accs, e_refs):
    return [e_refs[0][...] + accs[0]]


def _ep_swiglu(accs, e_refs):
    g, u = accs
    return [g * jax.nn.sigmoid(g) * u]


def _ep_rms(accs, e_refs):
    return [_rms(accs[0], e_refs[0][...])]


def _rope_slab(x, cos_t, sin_t):
    lane = lax.broadcasted_iota(jnp.int32, x.shape, 1)
    swapped = jnp.where(lane < QK_ROPE // 2,
                        pltpu.roll(x, LANES - QK_ROPE // 2, 1),
                        pltpu.roll(x, QK_ROPE // 2, 1))
    return x * cos_t + swapped * sin_t


def _ep_kv_down(accs, e_refs):
    g_ref, cos_ref, sin_ref = e_refs
    return [_rms(accs[0], g_ref[...]), _rope_slab(accs[1], cos_ref[...], sin_ref[...])]


def _ep_q_up(accs, e_refs):
    cos_ref, sin_ref = e_refs
    acc = accs[0]
    cos_t, sin_t = cos_ref[...], sin_ref[...]
    cols = []
    for h in range(acc.shape[1] // Q_HEAD_STRIDE):
        base = h * Q_HEAD_STRIDE
        cols.append(acc[:, base:base + LANES])
        cols.append(_rope_slab(acc[:, base + LANES:base + Q_HEAD_STRIDE], cos_t, sin_t))
    return [jnp.concatenate(cols, axis=1) * ATTN_SCALE]


def _rglru_kernel(ts, tc, gate_ref, rec_ref, cw_ref, cb_ref, wr_ref, br_ref, wi_ref, bi_ref,
                  lam_ref, o_ref, tail_ref, h_ref):
    t = pl.program_id(2)

    @pl.when(t == 0)
    def _():
        tail_ref[...] = jnp.zeros_like(tail_ref)
        h_ref[...] = jnp.zeros_like(h_ref)

    x = rec_ref[0].astype(F32)
    tail = tail_ref[...]
    cw = cw_ref[...]
    row8 = lax.broadcasted_iota(jnp.int32, (SUBLANES, tc), 0)
    u = x * cw[CONV_WIDTH - 1:CONV_WIDTH, :] + cb_ref[...]
    for k in range(1, CONV_WIDTH):
        xs = pltpu.roll(x, k, 0)
        top = jnp.where(row8 < k, pltpu.roll(tail, k, 0), xs[0:SUBLANES])
        xs = jnp.concatenate([top, xs[SUBLANES:]], axis=0)
        u = u + xs * cw[CONV_WIDTH - 1 - k:CONV_WIDTH - k, :]
    tail_ref[...] = x[ts - SUBLANES:ts]

    ub = u.astype(BF16)
    r_parts, i_parts = [], []
    for g in range(tc // LRU_BLOCK):
        ug = ub[:, g * LRU_BLOCK:(g + 1) * LRU_BLOCK]
        r_parts.append(jnp.dot(ug, wr_ref[g], preferred_element_type=F32))
        i_parts.append(jnp.dot(ug, wi_ref[g], preferred_element_type=F32))
    r = jax.nn.sigmoid(jnp.concatenate(r_parts, axis=1) + br_ref[...])
    ig = jax.nn.sigmoid(jnp.concatenate(i_parts, axis=1) + bi_ref[...])

    neg_lam = -lam_ref[...]
    softplus = jnp.maximum(neg_lam, 0.0) + jnp.log1p(jnp.exp(-jnp.abs(neg_lam)))
    log_a = (-LRU_C) * r * softplus
    a = jnp.exp(log_a)
    xin = (ig * u) * jnp.sqrt(-jnp.tanh(log_a) * (1.0 + a * a))

    row = lax.broadcasted_iota(jnp.int32, (ts, tc), 0)
    ca, cx = a, xin
    shift = 1
    while shift < ts:
        if shift < SUBLANES:
            keep = row >= shift
            a_s = jnp.where(keep, pltpu.roll(ca, shift, 0), 1.0)
            x_s = jnp.where(keep, pltpu.roll(cx, shift, 0), 0.0)
        else:
            a_s = jnp.concatenate([jnp.ones((shift, tc), F32), ca[:ts - shift]], axis=0)
            x_s = jnp.concatenate([jnp.zeros((shift, tc), F32), cx[:ts - shift]], axis=0)
        cx = ca * x_s + cx
        ca = ca * a_s
        shift *= 2
    h = ca * h_ref[SUBLANES - 1:SUBLANES, :] + cx
    h_ref[...] = h[ts - SUBLANES:ts]

    gate = gate_ref[0].astype(F32)
    o_ref[0] = (jax.nn.gelu(gate, approximate=True) * h).astype(o_ref.dtype)


def _rglru(gr, conv_w, conv_b, w_r, b_r, w_i, b_i, lam, batch, seq, *, ts=512, tc=512):
    d = conv_w.shape[1]
    ts, tc = min(ts, seq), min(tc, d)
    nc = d // tc
    gpb = tc // LRU_BLOCK
    gr3 = gr.reshape(batch, seq, 2 * d)
    vec = pl.BlockSpec((1, tc), lambda b, c, t: (0, c))
    wspec = pl.BlockSpec((gpb, LRU_BLOCK, LRU_BLOCK), lambda b, c, t: (c, 0, 0))
    out = pl.pallas_call(
        functools.partial(_rglru_kernel, ts, tc),
        out_shape=jax.ShapeDtypeStruct((batch, seq, d), BF16),
        grid=(batch, nc, seq // ts),
        in_specs=[
            pl.BlockSpec((1, ts, tc), lambda b, c, t: (b, t, c)),
            pl.BlockSpec((1, ts, tc), lambda b, c, t: (b, t, nc + c)),
            pl.BlockSpec((CONV_WIDTH, tc), lambda b, c, t: (0, c)),
            vec, wspec, vec, wspec, vec, vec,
        ],
        out_specs=pl.BlockSpec((1, ts, tc), lambda b, c, t: (b, t, c)),
        scratch_shapes=[pltpu.VMEM((SUBLANES, tc), F32), pltpu.VMEM((SUBLANES, tc), F32)],
        compiler_params=_cparams("parallel", "parallel", "arbitrary"),
        name="rglru",
    )(gr3, gr3, conv_w, conv_b.reshape(1, d), w_r.astype(BF16), b_r.reshape(1, d),
      w_i.astype(BF16), b_i.reshape(1, d), lam.reshape(1, d))
    return out.reshape(batch * seq, d)


def _attn_kernel(tq, q_ref, kn_ref, kr_ref, v_ref, o_ref, m_ref, l_ref, acc_ref):
    qi = pl.program_id(2)
    q = q_ref[0]
    qn, qr = q[:, :LANES], q[:, LANES:]
    m_ref[...] = jnp.full_like(m_ref, NEG_BIG)
    l_ref[...] = jnp.zeros_like(l_ref)
    acc_ref[...] = jnp.zeros_like(acc_ref)
    contract_last = (((1,), (1,)), ((), ()))

    def step(j, masked):
        k0 = pl.multiple_of(j * tq, tq)
        kn = kn_ref[0, pl.ds(k0, tq), :]
        kr = kr_ref[0, pl.ds(k0, tq), :]
        s = (lax.dot_general(qn, kn, contract_last, preferred_element_type=F32)
             + lax.dot_general(qr, kr, contract_last, preferred_element_type=F32))
        if masked:
            qc = lax.broadcasted_iota(jnp.int32, s.shape, 0) // CHUNK
            kc = lax.broadcasted_iota(jnp.int32, s.shape, 1) // CHUNK
            s = jnp.where(qc >= kc, s, NEG_BIG)
        m_prev = m_ref[...]
        m_next = jnp.maximum(m_prev, jnp.max(s, axis=-1, keepdims=True))
        p = jnp.exp(s - jnp.tile(m_next, (1, tq // LANES)))
        alpha = jnp.exp(m_prev - m_next)
        l_ref[...] = alpha * l_ref[...] + jnp.sum(p, axis=-1, keepdims=True)
        m_ref[...] = m_next
        pv = jnp.dot(p.astype(BF16), v_ref[0, pl.ds(k0, tq), :], preferred_element_type=F32)
        acc_ref[...] = alpha * acc_ref[...] + pv

    def body(j, carry):
        step(j, False)
        return carry

    lax.fori_loop(0, qi, body, 0)
    step(qi, True)
    o_ref[0] = (acc_ref[...] / l_ref[...]).astype(o_ref.dtype)


def _attention(q_cat, k_nope, k_rope, v, batch, seq, n_heads, *, tq=512):
    tq = min(tq, seq)
    q3 = q_cat.reshape(batch, seq, n_heads * Q_HEAD_STRIDE)
    kn3 = k_nope.reshape(batch, seq, n_heads * QK_NOPE)
    kr3 = k_rope.reshape(batch, seq, LANES)
    v3 = v.reshape(batch, seq, n_heads * V_DIM)
    out = pl.pallas_call(
        functools.partial(_attn_kernel, tq),
        out_shape=jax.ShapeDtypeStruct((batch, seq, n_heads * V_DIM), BF16),
        grid=(batch, n_heads, seq // tq),
        in_specs=[
            pl.BlockSpec((1, tq, Q_HEAD_STRIDE), lambda b, h, i: (b, i, h)),
            pl.BlockSpec((1, seq, QK_NOPE), lambda b, h, i: (b, 0, h)),
            pl.BlockSpec((1, seq, LANES), lambda b, h, i: (b, 0, 0)),
            pl.BlockSpec((1, seq, V_DIM), lambda b, h, i: (b, 0, h)),
        ],
        out_specs=pl.BlockSpec((1, tq, V_DIM), lambda b, h, i: (b, i, h)),
        scratch_shapes=[pltpu.VMEM((tq, LANES), F32), pltpu.VMEM((tq, LANES), F32),
                        pltpu.VMEM((tq, V_DIM), F32)],
        compiler_params=_cparams("parallel", "parallel", "arbitrary"),
        name="attention",
    )(q3, kn3, kr3, v3)
    return out.reshape(batch * seq, n_heads * V_DIM)


def _moe_router_kernel(x_ref, g_ref, wr_ref, xn_ref, idx_ref, gate_ref):
    x = x_ref[...]
    xn = _rms(x, g_ref[...])
    xn_ref[...] = xn
    logits = jnp.dot(xn, wr_ref[...], preferred_element_type=F32,
                     precision=lax.Precision.HIGHEST)
    n_e = logits.shape[1]
    lane = lax.broadcasted_iota(jnp.int32, logits.shape, 1)
    m1 = jnp.max(logits, axis=-1, keepdims=True)
    i1 = jnp.min(jnp.where(logits == m1, lane, n_e), axis=-1, keepdims=True)
    rest = jnp.where(lane == i1, -jnp.inf, logits)
    m2 = jnp.max(rest, axis=-1, keepdims=True)
    i2 = jnp.min(jnp.where(rest == m2, lane, n_e), axis=-1, keepdims=True)
    e2 = jnp.exp(m2 - m1)
    denom = 1.0 + e2
    slot = lax.broadcasted_iota(jnp.int32, idx_ref.shape, 1)
    idx_ref[...] = jnp.where(slot == 0, i1, i2)
    gate_ref[...] = jnp.where(slot == 0, 1.0 / denom, e2 / denom)


def _moe_router(x, g, w_router, tm=256):
    m, d = x.shape
    tm = min(tm, m)
    n_e = w_router.shape[1]
    row = pl.BlockSpec((tm, d), lambda i: (i, 0))
    small = pl.BlockSpec((tm, TOP_K), lambda i: (i, 0))
    return pl.pallas_call(
        _moe_router_kernel,
        out_shape=[jax.ShapeDtypeStruct((m, d), F32),
                   jax.ShapeDtypeStruct((m, TOP_K), jnp.int32),
                   jax.ShapeDtypeStruct((m, TOP_K), F32)],
        grid=(m // tm,),
        in_specs=[row, pl.BlockSpec((1, d), lambda i: (0, 0)),
                  pl.BlockSpec((d, n_e), lambda i: (0, 0))],
        out_specs=[row, small, small],
        compiler_params=_cparams("parallel"),
        name="moe_norm_router",
    )(x, g.reshape(1, d), w_router)


def _row_copy(src_hbm, src_row, buf, dst_row, sem):
    return pltpu.make_async_copy(src_hbm.at[pl.ds(src_row, 1)], buf.at[pl.ds(dst_row, 1)], sem)


def _moe_gather_kernel(blk, tok_ref, nused_ref, x_hbm, o_ref, buf, sem):
    b = pl.program_id(0)

    @pl.when(b < nused_ref[0])
    def _():
        def issue(r, c):
            _row_copy(x_hbm, tok_ref[b * blk + r], buf, r, sem).start()
            return c

        lax.fori_loop(0, blk, issue, 0)

        def drain(r, c):
            _row_copy(x_hbm, 0, buf, r, sem).wait()
            return c

        lax.fori_loop(0, blk, drain, 0)
        o_ref[...] = buf[...].astype(o_ref.dtype)

    @pl.when(b >= nused_ref[0])
    def _():
        o_ref[...] = jnp.zeros_like(o_ref)


def _moe_gather(xn, row_tok, n_used, blk):
    n_rows = row_tok.shape[0]
    d = xn.shape[1]
    return pl.pallas_call(
        functools.partial(_moe_gather_kernel, blk),
        out_shape=jax.ShapeDtypeStruct((n_rows, d), BF16),
        grid_spec=pltpu.PrefetchScalarGridSpec(
            num_scalar_prefetch=2,
            grid=(n_rows // blk,),
            in_specs=[pl.BlockSpec(memory_space=pl.ANY)],
            out_specs=pl.BlockSpec((blk, d), lambda b, tok, nu: (b, 0)),
            scratch_shapes=[pltpu.VMEM((blk, d), F32), pltpu.SemaphoreType.DMA],
        ),
        compiler_params=_cparams("arbitrary"),
        name="moe_gather",
    )(row_tok, n_used, xn)


def _moe_up_kernel(be_ref, nused_ref, x_ref, wg_ref, wu_ref, o_ref):
    b = pl.program_id(0)

    @pl.when(b < nused_ref[0])
    def _():
        x = x_ref[...]
        g = jnp.dot(x, wg_ref[0], preferred_element_type=F32)
        u = jnp.dot(x, wu_ref[0], preferred_element_type=F32)
        o_ref[...] = (g * jax.nn.sigmoid(g) * u).astype(o_ref.dtype)

    @pl.when(b >= nused_ref[0])
    def _():
        o_ref[...] = jnp.zeros_like(o_ref)


def _moe_down_kernel(be_ref, nused_ref, h_ref, wd_ref, o_ref):
    b = pl.program_id(0)

    @pl.when(b < nused_ref[0])
    def _():
        o_ref[...] = jnp.dot(h_ref[...], wd_ref[0], preferred_element_type=F32)

    @pl.when(b >= nused_ref[0])
    def _():
        o_ref[...] = jnp.zeros_like(o_ref)


def _moe_experts(x_rows, block_e, n_used, w_gate, w_up, w_down, blk, *, tf=1024, tn=1024):
    n_rows, d = x_rows.shape
    f = w_gate.shape[2]
    tf, tn = min(tf, f), min(tn, d)
    nf, nn = f // tf, d // tn

    def wcol(nj):
        return lambda b, j, be, nu: (be[b], 0, jnp.where(b < nu[0], j, nj - 1))

    h_rows = pl.pallas_call(
        _moe_up_kernel,
        out_shape=jax.ShapeDtypeStruct((n_rows, f), BF16),
        grid_spec=pltpu.PrefetchScalarGridSpec(
            num_scalar_prefetch=2,
            grid=(n_rows // blk, nf),
            in_specs=[pl.BlockSpec((blk, d), lambda b, j, be, nu: (b, 0)),
                      pl.BlockSpec((1, d, tf), wcol(nf)),
                      pl.BlockSpec((1, d, tf), wcol(nf))],
            out_specs=pl.BlockSpec((blk, tf), lambda b, j, be, nu: (b, j)),
        ),
        compiler_params=_cparams("arbitrary", "arbitrary"),
        name="moe_gate_up",
    )(block_e, n_used, x_rows, w_gate, w_up)
    y_rows = pl.pallas_call(
        _moe_down_kernel,
        out_shape=jax.ShapeDtypeStruct((n_rows, d), F32),
        grid_spec=pltpu.PrefetchScalarGridSpec(
            num_scalar_prefetch=2,
            grid=(n_rows // blk, nn),
            in_specs=[pl.BlockSpec((blk, f), lambda b, j, be, nu: (b, 0)),
                      pl.BlockSpec((1, f, tn), wcol(nn))],
            out_specs=pl.BlockSpec((blk, tn), lambda b, j, be, nu: (b, j)),
        ),
        compiler_params=_cparams("arbitrary", "arbitrary"),
        name="moe_down",
    )(block_e, n_used, h_rows, w_down)
    return y_rows


def _moe_combine_kernel(tt, pos_ref, x_ref, gate_ref, g_ref, y_hbm, o_ref, buf, sem):
    i = pl.program_id(0)

    def issue(r, c):
        for s in range(TOP_K):
            _row_copy(y_hbm, pos_ref[(i * tt + r) * TOP_K + s], buf.at[s], r, sem).start()
        return c

    lax.fori_loop(0, tt, issue, 0)

    def drain(r, c):
        for s in range(TOP_K):
            _row_copy(y_hbm, 0, buf.at[s], r, sem).wait()
        return c

    lax.fori_loop(0, tt, drain, 0)
    gates = gate_ref[...]
    y = buf[0] * gates[:, 0:1] + buf[1] * gates[:, 1:2]
    o_ref[...] = _rms(x_ref[...] + y, g_ref[...])


def _moe_combine(x, y_rows, pos, gates, g, tt=256):
    m, d = x.shape
    tt = min(tt, m)
    row = lambda i, p: (i, 0)
    return pl.pallas_call(
        functools.partial(_moe_combine_kernel, tt),
        out_shape=jax.ShapeDtypeStruct((m, d), F32),
        grid_spec=pltpu.PrefetchScalarGridSpec(
            num_scalar_prefetch=1,
            grid=(m // tt,),
            in_specs=[pl.BlockSpec((tt, d), row),
                      pl.BlockSpec((tt, TOP_K), row),
                      pl.BlockSpec((1, d), lambda i, p: (0, 0)),
                      pl.BlockSpec(memory_space=pl.ANY)],
            out_specs=pl.BlockSpec((tt, d), row),
            scratch_shapes=[pltpu.VMEM((TOP_K, tt, d), F32), pltpu.SemaphoreType.DMA],
        ),
        compiler_params=_cparams("arbitrary"),
        name="moe_combine_norm",
    )(pos, x, gates, g.reshape(1, d), y_rows)


def _moe_routing(idx, n_experts, blk):
    n = idx.shape[0]
    flat_e = idx.reshape(-1)
    onehot = (flat_e[:, None] == jnp.arange(n_experts, dtype=jnp.int32)[None, :]).astype(jnp.int32)
    csum = jnp.cumsum(onehot, axis=0)
    rank = jnp.sum(csum * onehot, axis=1) - 1
    counts = csum[-1]
    padded = (counts + blk - 1) // blk * blk
    pend = jnp.cumsum(padded)
    pstart = pend - padded
    dest = (pstart[flat_e] + rank).astype(jnp.int32)
    n_rows = -(-(n * TOP_K + n_experts * (blk - 1)) // blk) * blk
    n_blocks = n_rows // blk
    flat_tok = jnp.arange(n * TOP_K, dtype=jnp.int32) // TOP_K
    row_tok = jnp.zeros((n_rows,), jnp.int32).at[dest].set(flat_tok)
    block_e = jnp.minimum(
        jnp.searchsorted(pend, jnp.arange(n_blocks, dtype=jnp.int32) * blk, side='right'),
        n_experts - 1).astype(jnp.int32)
    n_used = (pend[-1] // blk).astype(jnp.int32).reshape(1)
    return dest, row_tok, block_e, n_used


def _rope_tables(seq):
    inv_freq = ROPE_THETA ** (-jnp.arange(0, QK_ROPE, 2, dtype=F32) / QK_ROPE)
    ang = jnp.arange(seq, dtype=F32)[:, None] * inv_freq[None, :]
    cos, sin = jnp.cos(ang), jnp.sin(ang)
    pad = jnp.zeros((seq, LANES - QK_ROPE), F32)
    return (jnp.concatenate([cos, cos, pad], axis=1),
            jnp.concatenate([-sin, sin, pad], axis=1))


def kernel(x, a_norm, a_w_in, a_conv_w, a_conv_b, a_w_rgate, a_b_rgate, a_w_igate, a_b_igate, a_lambda, a_w_out, kv_norm, kv_w_down, kv_latent_norm, kv_w_rope, kv_w_up_k, kv_w_up_v, b_norm, b_w_q_down, b_q_latent_norm, b_w_q_up, b_w_q_rope, b_w_out, ffn_norm, ffn_w_gate, ffn_w_up, ffn_w_down, moe_norm, moe_w_router, moe_w_gate, moe_w_up, moe_w_down, final_norm):
    batch, seq, d = x.shape
    n = batch * seq
    n_heads = kv_w_up_k.shape[1]
    assert a_norm.shape[0] == 1 and b_norm.shape[0] == 1, "depth-2 block: one RG-LRU and one MLA layer"
    xf = x.reshape(n, d)
    tm = min(1024, seq)
    n_seq_tiles = seq // tm

    def mm_residual(a, w, res, name):
        tn = min(512, w.shape[1])
        return _matmul(a, [w], [tn], _ep_residual,
                       [(res, (tm, tn), lambda i, j: (i, j))], [(w.shape[1], F32, tn)],
                       tm=tm, name=name)[0]

    xn, = _norm(xf, [a_norm[0]])
    w_in = a_w_in[0].astype(BF16)
    tn = min(1024, w_in.shape[1])
    gr, = _matmul(xn, [w_in], [tn], _ep_identity, [], [(w_in.shape[1], BF16, tn)], tm=tm, name="rglru_in")
    gated = _rglru(gr, a_conv_w[0], a_conv_b[0], a_w_rgate[0], a_b_rgate[0], a_w_igate[0],
                   a_b_igate[0], a_lambda[0], batch, seq)
    x1 = mm_residual(gated, a_w_out[0].astype(BF16), xf, "rglru_out")

    xn, = _norm(x1, [ffn_norm[0]])
    wg, wu, wd = (w[0].astype(BF16) for w in (ffn_w_gate, ffn_w_up, ffn_w_down))
    f = wg.shape[1]
    tf = min(512, f)
    hid, = _matmul(xn, [wg, wu], [tf, tf], _ep_swiglu, [], [(f, BF16, tf)], tm=tm, name="ffn_gate_up")
    tn = min(1024, d)
    x2, = _matmul(hid, [wd], [tn], _ep_residual, [(x1, (tm, tn), lambda i, j: (i, j))],
                  [(d, F32, tn)], tm=tm, tk=min(2048, f), name="ffn_down")

    cos_t, sin_t = _rope_tables(seq)
    rope_specs = [(cos_t, (tm, LANES), lambda i, j: (i % n_seq_tiles, 0)),
                  (sin_t, (tm, LANES), lambda i, j: (i % n_seq_tiles, 0))]
    h_kv, xn_b = _norm(x2, [kv_norm, b_norm[0]])
    kv_rank = kv_w_down.shape[1]
    w_kv_rope = jnp.pad(kv_w_rope, ((0, 0), (0, LANES - QK_ROPE))).astype(BF16)
    c_kv, k_rope = _matmul(
        h_kv, [kv_w_down.astype(BF16), w_kv_rope], [kv_rank, LANES], _ep_kv_down,
        [(kv_latent_norm.reshape(1, kv_rank), (1, kv_rank), lambda i, j: (0, 0))] + rope_specs,
        [(kv_rank, BF16, kv_rank), (LANES, BF16, LANES)], tm=tm, name="kv_down")
    hk = n_heads * QK_NOPE
    tn = min(1024, hk)
    k_nope, v = _matmul(
        c_kv, [kv_w_up_k.reshape(kv_rank, hk).astype(BF16),
               kv_w_up_v.reshape(kv_rank, n_heads * V_DIM).astype(BF16)],
        [tn, tn], _ep_identity, [], [(hk, BF16, tn), (n_heads * V_DIM, BF16, tn)],
        tm=tm, name="kv_up")

    q_rank = b_w_q_down.shape[2]
    c_q, = _matmul(xn_b, [b_w_q_down[0].astype(BF16)], [q_rank], _ep_rms,
                   [(b_q_latent_norm[0].reshape(1, q_rank), (1, q_rank), lambda i, j: (0, 0))],
                   [(q_rank, BF16, q_rank)], tm=tm, name="q_down")
    w_q_cat = jnp.concatenate(
        [b_w_q_up[0], b_w_q_rope[0],
         jnp.zeros((q_rank, n_heads, Q_HEAD_STRIDE - QK_NOPE - QK_ROPE), F32)],
        axis=-1).reshape(q_rank, n_heads * Q_HEAD_STRIDE).astype(BF16)
    tn = min(1024, n_heads * Q_HEAD_STRIDE)
    q_cat, = _matmul(c_q, [w_q_cat], [tn], _ep_q_up, rope_specs,
                     [(n_heads * Q_HEAD_STRIDE, BF16, tn)], tm=tm, name="q_up")
    o = _attention(q_cat, k_nope, k_rope, v, batch, seq, n_heads)
    x3 = mm_residual(o, b_w_out[0].astype(BF16), x2, "attn_out")

    n_experts = moe_w_router.shape[2]
    xn_moe, top_idx, gates = _moe_router(x3, moe_norm[0], moe_w_router[0])
    dest, row_tok, block_e, n_used = _moe_routing(top_idx, n_experts, MOE_BLOCK)
    x_rows = _moe_gather(xn_moe, row_tok, n_used, MOE_BLOCK)
    y_rows = _moe_experts(x_rows, block_e, n_used, moe_w_gate[0].astype(BF16),
                          moe_w_up[0].astype(BF16), moe_w_down[0].astype(BF16), MOE_BLOCK)
    out = _moe_combine(x3, y_rows, dest, gates, final_norm)
    return out.reshape(batch, seq, d)
```

```python
import functools

import jax
import jax.numpy as jnp
from jax import lax
from jax.experimental import pallas as pl
from jax.experimental.pallas import tpu as pltpu

F32 = jnp.float32
BF16 = jnp.bfloat16

CHUNK = 64
LRU_BLOCK = 256
CONV_WIDTH = 4
LRU_C = 8.0
QK_NOPE = 128
QK_ROPE = 64
V_DIM = 128
ROPE_THETA = 10000.0
ATTN_SCALE = (QK_NOPE + QK_ROPE) ** -0.5
TOP_K = 2
EPS = 1e-6

VMEM_LIMIT_BYTES = 56 * 1024 * 1024
LANES = 128
SUBLANES = 8
BF16_SUBLANES = 16
Q_HEAD_STRIDE = 2 * LANES
NEG_BIG = -1e30
LOG2_E = 1.4426950408889634

MOE_BLOCK = 512


def _cparams(*sem):
    return pltpu.CompilerParams(dimension_semantics=sem, vmem_limit_bytes=VMEM_LIMIT_BYTES)


def _rms(x, g):
    return x * lax.rsqrt(jnp.mean(x * x, axis=-1, keepdims=True) + EPS) * g


def _norm_kernel(n_out, x_ref, *refs):
    g_refs, o_refs = refs[:n_out], refs[n_out:]
    x = x_ref[...]
    y = x * lax.rsqrt(jnp.mean(x * x, axis=-1, keepdims=True) + EPS)
    for g_ref, o_ref in zip(g_refs, o_refs):
        o_ref[...] = (y * g_ref[...]).astype(o_ref.dtype)


def _norm(x, gains, out_dtype=BF16, tm=256):
    m, d = x.shape
    tm = min(tm, m)
    n = len(gains)
    row = pl.BlockSpec((tm, d), lambda i: (i, 0))
    vec = pl.BlockSpec((1, d), lambda i: (0, 0))
    outs = pl.pallas_call(
        functools.partial(_norm_kernel, n),
        out_shape=[jax.ShapeDtypeStruct((m, d), out_dtype)] * n,
        grid=(m // tm,),
        in_specs=[row] + [vec] * n,
        out_specs=[row] * n,
        compiler_params=_cparams("parallel"),
        name="rmsnorm",
    )(x, *[g.reshape(1, d) for g in gains])
    return outs


def _mm_kernel(nw, ne, nr, no, nk, epilogue, a_ref, *refs):
    w_refs = refs[:nw]
    e_refs = refs[nw:nw + ne]
    rin_refs = refs[nw + ne:nw + ne + nr]
    o_refs = refs[nw + ne + nr:nw + ne + nr + no]
    rout_refs = refs[nw + ne + nr + no:nw + ne + 2 * nr + no]
    acc_refs = refs[nw + ne + 2 * nr + no:]

    for rin, rout in zip(rin_refs, rout_refs):
        rout[...] = rin[...].astype(rout.dtype)

    def finish(accs):
        vals = epilogue(accs, e_refs)
        for o_ref, v in zip(o_refs, vals):
            o_ref[...] = v.astype(o_ref.dtype)

    if nk == 1:
        finish([jnp.dot(a_ref[...], w[...], preferred_element_type=F32) for w in w_refs])
        return

    k = pl.program_id(2)

    @pl.when(k == 0)
    def _():
        for acc in acc_refs:
            acc[...] = jnp.zeros_like(acc)

    for acc, w in zip(acc_refs, w_refs):
        acc[...] += jnp.dot(a_ref[...], w[...], preferred_element_type=F32)

    @pl.when(k == nk - 1)
    def _():
        finish([acc[...] for acc in acc_refs])


def _rider_slab(rows, n_steps):
    slab = -(-rows // n_steps)
    slab = -(-slab // BF16_SUBLANES) * BF16_SUBLANES
    while rows % slab:
        slab += BF16_SUBLANES
    return slab


def _matmul(a, ws, tns, epilogue, extras, outs, *, tm, tk=None, riders=(), name):
    m, kdim = a.shape
    tm = min(tm, m)
    tk = kdim if tk is None else min(tk, kdim)
    nk = kdim // tk
    nj = ws[0].shape[1] // tns[0]
    n_steps = (m // tm) * nj * nk
    in_specs = [pl.BlockSpec((tm, tk), lambda i, j, k: (i, k))]
    for w, tn in zip(ws, tns):
        assert w.shape[1] // tn == nj
        in_specs.append(pl.BlockSpec((tk, tn), lambda i, j, k: (k, j)))
    for _, bs, im in extras:
        in_specs.append(pl.BlockSpec(bs, functools.partial(lambda im, i, j, k: im(i, j), im)))
    out_shape = [jax.ShapeDtypeStruct((m, n), dt) for n, dt, _ in outs]
    out_specs = [pl.BlockSpec((tm, tn), lambda i, j, k: (i, j)) for _, _, tn in outs]
    rider_specs = []
    for r in riders:
        rows, cols = r.shape
        slab = _rider_slab(rows, n_steps)
        rider_specs.append(pl.BlockSpec(
            (slab, cols),
            functools.partial(lambda last, i, j, k: (jnp.minimum((i * nj + j) * nk + k, last), 0),
                              rows // slab - 1)))
        out_shape.append(jax.ShapeDtypeStruct((rows, cols), BF16))
    scratch = [pltpu.VMEM((tm, tn), F32) for tn in tns] if nk > 1 else []
    semantics = ("arbitrary",) * 3 if riders else ("parallel", "parallel", "arbitrary")
    res = pl.pallas_call(
        functools.partial(_mm_kernel, len(ws), len(extras), len(riders), len(outs), nk, epilogue),
        out_shape=out_shape,
        grid=(m // tm, nj, nk),
        in_specs=in_specs + rider_specs,
        out_specs=out_specs + rider_specs,
        scratch_shapes=scratch,
        compiler_params=_cparams(*semantics),
        name=name,
    )(a, *ws, *[e[0] for e in extras], *riders)
    return res


def _ep_identity(accs, e_refs):
    return accs


def _ep_residual(accs, e_refs):
    return [e_refs[0][...] + accs[0]]


def _ep_swiglu(accs, e_refs):
    g, u = accs
    return [g * jax.nn.sigmoid(g) * u]


def _ep_rms(accs, e_refs):
    return [_rms(accs[0], e_refs[0][...])]


def _rope_slab(x, cos_t, sin_t):
    lane = lax.broadcasted_iota(jnp.int32, x.shape, 1)
    swapped = jnp.where(lane < QK_ROPE // 2,
                        pltpu.roll(x, LANES - QK_ROPE // 2, 1),
                        pltpu.roll(x, QK_ROPE // 2, 1))
    return x * cos_t + swapped * sin_t


def _ep_kv_down(accs, e_refs):
    g_ref, cos_ref, sin_ref = e_refs
    return [_rms(accs[0], g_ref[...]), _rope_slab(accs[1], cos_ref[...], sin_ref[...])]


def _ep_q_up(accs, e_refs):
    cos_ref, sin_ref = e_refs
    acc = accs[0]
    cos_t, sin_t = cos_ref[...], sin_ref[...]
    cols = []
    for h in range(acc.shape[1] // Q_HEAD_STRIDE):
        base = h * Q_HEAD_STRIDE
        cols.append(acc[:, base:base + LANES])
        cols.append(_rope_slab(acc[:, base + LANES:base + Q_HEAD_STRIDE], cos_t, sin_t))
    return [jnp.concatenate(cols, axis=1) * (ATTN_SCALE * LOG2_E)]


def _ep_kv_up(accs, e_refs):
    acc_k, acc_v = accs
    k_rope = e_refs[0][...].astype(F32)
    cols = []
    for h in range(acc_k.shape[1] // QK_NOPE):
        cols.append(acc_k[:, h * QK_NOPE:(h + 1) * QK_NOPE])
        cols.append(k_rope)
    return [jnp.concatenate(cols, axis=1), acc_v]


def _rglru_kernel(ts, tc, gate_ref, rec_ref, cw_ref, cb_ref, wr_ref, br_ref, wi_ref, bi_ref,
                  lam_ref, o_ref, tail_ref, h_ref, a_scr, x_scr):
    t = pl.program_id(2)

    @pl.when(t == 0)
    def _():
        tail_ref[...] = jnp.zeros_like(tail_ref)
        h_ref[...] = jnp.zeros_like(h_ref)

    x = rec_ref[0].astype(F32)
    tail = tail_ref[...]
    cw = cw_ref[...]
    row8 = lax.broadcasted_iota(jnp.int32, (SUBLANES, tc), 0)
    u = x * cw[CONV_WIDTH - 1:CONV_WIDTH, :] + cb_ref[...]
    for k in range(1, CONV_WIDTH):
        xs = pltpu.roll(x, k, 0)
        top = jnp.where(row8 < k, pltpu.roll(tail, k, 0), xs[0:SUBLANES])
        xs = jnp.concatenate([top, xs[SUBLANES:]], axis=0)
        u = u + xs * cw[CONV_WIDTH - 1 - k:CONV_WIDTH - k, :]
    tail_ref[...] = x[ts - SUBLANES:ts]

    ub = u.astype(BF16)
    r_parts, i_parts = [], []
    for g in range(tc // LRU_BLOCK):
        ug = ub[:, g * LRU_BLOCK:(g + 1) * LRU_BLOCK]
        r_parts.append(jnp.dot(ug, wr_ref[g], preferred_element_type=F32))
        i_parts.append(jnp.dot(ug, wi_ref[g], preferred_element_type=F32))
    r = jax.nn.sigmoid(jnp.concatenate(r_parts, axis=1) + br_ref[...])
    ig = jax.nn.sigmoid(jnp.concatenate(i_parts, axis=1) + bi_ref[...])

    neg_lam = -lam_ref[...]
    softplus = jnp.maximum(neg_lam, 0.0) + jnp.log1p(jnp.exp(-jnp.abs(neg_lam)))
    log_a = (-LRU_C) * r * softplus
    a = jnp.exp(log_a)
    z = -jnp.tanh(log_a) * (1.0 + a * a)
    xin = (ig * u) * jnp.where(z > 0.0, z * lax.rsqrt(z), 0.0)

    nb = ts // SUBLANES
    grow = lax.broadcasted_iota(jnp.int32, (nb, LANES), 0)
    gate = jax.nn.gelu(gate_ref[0].astype(F32), approximate=True)
    for c in range(tc // LANES):
        lanes = slice(c * LANES, (c + 1) * LANES)
        a_scr[c] = a[:, lanes]
        x_scr[c] = xin[:, lanes]
        decay, state = [], []
        for r in range(SUBLANES):
            a_r = a_scr[c, pl.ds(r, nb, stride=SUBLANES), :]
            x_r = x_scr[c, pl.ds(r, nb, stride=SUBLANES), :]
            decay.append(a_r if r == 0 else a_r * decay[-1])
            state.append(x_r if r == 0 else a_r * state[-1] + x_r)
        ca, cx = decay[-1], state[-1]
        shift = 1
        while shift < nb:
            if shift < SUBLANES:
                keep = grow >= shift
                a_s = jnp.where(keep, pltpu.roll(ca, shift, 0), 1.0)
                x_s = jnp.where(keep, pltpu.roll(cx, shift, 0), 0.0)
            else:
                a_s = jnp.concatenate([jnp.ones((shift, LANES), F32), ca[:nb - shift]], axis=0)
                x_s = jnp.concatenate([jnp.zeros((shift, LANES), F32), cx[:nb - shift]], axis=0)
            cx = ca * x_s + cx
            ca = ca * a_s
            shift *= 2
        h_prev = h_ref[0:1, lanes]
        group_out = ca * h_prev + cx
        group_in = jnp.where(grow == 0, h_prev, pltpu.roll(group_out, 1, 0))
        for r in range(SUBLANES):
            x_scr[c, pl.ds(r, nb, stride=SUBLANES), :] = decay[r] * group_in + state[r]
        h_ref[:, lanes] = jnp.broadcast_to(group_out[nb - 1:nb, :], (SUBLANES, LANES))
        o_ref[0, :, lanes] = (gate[:, lanes] * x_scr[c]).astype(o_ref.dtype)


def _rglru(gr, conv_w, conv_b, w_r, b_r, w_i, b_i, lam, batch, seq, *, ts=512, tc=512):
    d = conv_w.shape[1]
    ts, tc = min(ts, seq), min(tc, d)
    nc = d // tc
    gpb = tc // LRU_BLOCK
    gr3 = gr.reshape(batch, seq, 2 * d)
    vec = pl.BlockSpec((1, tc), lambda b, c, t: (0, c))
    wspec = pl.BlockSpec((gpb, LRU_BLOCK, LRU_BLOCK), lambda b, c, t: (c, 0, 0))
    out = pl.pallas_call(
        functools.partial(_rglru_kernel, ts, tc),
        out_shape=jax.ShapeDtypeStruct((batch, seq, d), BF16),
        grid=(batch, nc, seq // ts),
        in_specs=[
            pl.BlockSpec((1, ts, tc), lambda b, c, t: (b, t, c)),
            pl.BlockSpec((1, ts, tc), lambda b, c, t: (b, t, nc + c)),
            pl.BlockSpec((CONV_WIDTH, tc), lambda b, c, t: (0, c)),
            vec, wspec, vec, wspec, vec, vec,
        ],
        out_specs=pl.BlockSpec((1, ts, tc), lambda b, c, t: (b, t, c)),
        scratch_shapes=[pltpu.VMEM((SUBLANES, tc), F32), pltpu.VMEM((SUBLANES, tc), F32),
                        pltpu.VMEM((tc // LANES, ts, LANES), F32),
                        pltpu.VMEM((tc // LANES, ts, LANES), F32)],
        compiler_params=_cparams("parallel", "parallel", "arbitrary"),
        name="rglru",
    )(gr3, gr3, conv_w, conv_b.reshape(1, d), w_r.astype(BF16), b_r.reshape(1, d),
      w_i.astype(BF16), b_i.reshape(1, d), lam.reshape(1, d))
    return out.reshape(batch * seq, d)


def _attn_kernel(tq, hps, q_ref, k_ref, v_ref, o_ref, m_ref, l_ref, acc_ref):
    qi = pl.program_id(2)
    m_ref[...] = jnp.full_like(m_ref, NEG_BIG)
    l_ref[...] = jnp.zeros_like(l_ref)
    acc_ref[...] = jnp.zeros_like(acc_ref)
    contract_last = (((1,), (1,)), ((), ()))

    def step(j, masked):
        k0 = pl.multiple_of(j * tq, tq)
        for h in range(hps):
            q = q_ref[0, :, h * Q_HEAD_STRIDE:(h + 1) * Q_HEAD_STRIDE]
            k = k_ref[0, pl.ds(k0, tq), h * Q_HEAD_STRIDE:(h + 1) * Q_HEAD_STRIDE]
            s = lax.dot_general(q, k, contract_last, preferred_element_type=F32)
            if masked:
                qc = lax.broadcasted_iota(jnp.int32, s.shape, 0) // CHUNK
                kc = lax.broadcasted_iota(jnp.int32, s.shape, 1) // CHUNK
                s = jnp.where(qc >= kc, s, NEG_BIG)
            m_prev = m_ref[h]
            m_next = jnp.maximum(m_prev, jnp.max(s, axis=-1, keepdims=True))
            p = jnp.exp2(s - jnp.tile(m_next, (1, tq // LANES)))
            alpha = jnp.exp2(m_prev - m_next)
            l_ref[h] = alpha * l_ref[h] + jnp.sum(p, axis=-1, keepdims=True)
            m_ref[h] = m_next
            v = v_ref[0, pl.ds(k0, tq), h * V_DIM:(h + 1) * V_DIM]
            acc_ref[h] = alpha * acc_ref[h] + jnp.dot(p.astype(BF16), v, preferred_element_type=F32)

    def body(j, carry):
        step(j, False)
        return carry

    lax.fori_loop(0, qi, body, 0)
    step(qi, True)
    for h in range(hps):
        o_ref[0, :, h * V_DIM:(h + 1) * V_DIM] = (acc_ref[h] / l_ref[h]).astype(o_ref.dtype)


def _attention(q_cat, k_cat, v, batch, seq, n_heads, *, tq=512, hps=2):
    tq = min(tq, seq)
    q3 = q_cat.reshape(batch, seq, n_heads * Q_HEAD_STRIDE)
    k3 = k_cat.reshape(batch, seq, n_heads * Q_HEAD_STRIDE)
    v3 = v.reshape(batch, seq, n_heads * V_DIM)
    out = pl.pallas_call(
        functools.partial(_attn_kernel, tq, hps),
        out_shape=jax.ShapeDtypeStruct((batch, seq, n_heads * V_DIM), BF16),
        grid=(batch, n_heads // hps, seq // tq),
        in_specs=[
            pl.BlockSpec((1, tq, hps * Q_HEAD_STRIDE), lambda b, h, i: (b, i, h)),
            pl.BlockSpec((1, seq, hps * Q_HEAD_STRIDE), lambda b, h, i: (b, 0, h)),
            pl.BlockSpec((1, seq, hps * V_DIM), lambda b, h, i: (b, 0, h)),
        ],
        out_specs=pl.BlockSpec((1, tq, hps * V_DIM), lambda b, h, i: (b, i, h)),
        scratch_shapes=[pltpu.VMEM((hps, tq, LANES), F32), pltpu.VMEM((hps, tq, LANES), F32),
                        pltpu.VMEM((hps, tq, V_DIM), F32)],
        compiler_params=_cparams("parallel", "parallel", "arbitrary"),
        name="attention",
    )(q3, k3, v3)
    return out.reshape(batch * seq, n_heads * V_DIM)


def _moe_router_kernel(x_ref, g_ref, wr_ref, xn_ref, idx_ref, gate_ref):
    x = x_ref[...]
    xn = _rms(x, g_ref[...])
    xn_ref[...] = xn
    logits = jnp.dot(xn, wr_ref[...], preferred_element_type=F32,
                     precision=lax.Precision.HIGHEST)
    n_e = logits.shape[1]
    lane = lax.broadcasted_iota(jnp.int32, logits.shape, 1)
    m1 = jnp.max(logits, axis=-1, keepdims=True)
    i1 = jnp.min(jnp.where(logits == m1, lane, n_e), axis=-1, keepdims=True)
    rest = jnp.where(lane == i1, -jnp.inf, logits)
    m2 = jnp.max(rest, axis=-1, keepdims=True)
    i2 = jnp.min(jnp.where(rest == m2, lane, n_e), axis=-1, keepdims=True)
    e2 = jnp.exp(m2 - m1)
    denom = 1.0 + e2
    slot = lax.broadcasted_iota(jnp.int32, idx_ref.shape, 1)
    idx_ref[...] = jnp.where(slot == 0, i1, i2)
    gate_ref[...] = jnp.where(slot == 0, 1.0 / denom, e2 / denom)


def _moe_router(x, g, w_router, tm=256):
    m, d = x.shape
    tm = min(tm, m)
    n_e = w_router.shape[1]
    row = pl.BlockSpec((tm, d), lambda i: (i, 0))
    small = pl.BlockSpec((tm, TOP_K), lambda i: (i, 0))
    return pl.pallas_call(
        _moe_router_kernel,
        out_shape=[jax.ShapeDtypeStruct((m, d), F32),
                   jax.ShapeDtypeStruct((m, TOP_K), jnp.int32),
                   jax.ShapeDtypeStruct((m, TOP_K), F32)],
        grid=(m // tm,),
        in_specs=[row, pl.BlockSpec((1, d), lambda i: (0, 0)),
                  pl.BlockSpec((d, n_e), lambda i: (0, 0))],
        out_specs=[row, small, small],
        compiler_params=_cparams("parallel"),
        name="moe_norm_router",
    )(x, g.reshape(1, d), w_router)


def _row_copy(src_hbm, src_row, buf, dst_row, sem):
    return pltpu.make_async_copy(src_hbm.at[pl.ds(src_row, 1)], buf.at[pl.ds(dst_row, 1)], sem)


DMA_LOOP_UNROLL = 8


def _moe_gather_kernel(blk, tok_ref, nused_ref, x_hbm, o_ref, buf, sem):
    b = pl.program_id(0)
    n_used = nused_ref[0]

    def fetch(block, slot):
        def issue(r, c):
            _row_copy(x_hbm, tok_ref[block * blk + r], buf.at[slot], r, sem.at[slot]).start()
            return c

        lax.fori_loop(0, blk, issue, 0, unroll=DMA_LOOP_UNROLL)

    @pl.when((b == 0) & (n_used > 0))
    def _():
        fetch(0, 0)

    @pl.when(b + 1 < n_used)
    def _():
        fetch(b + 1, (b + 1) % 2)

    @pl.when(b < n_used)
    def _():
        slot = b % 2

        def drain(r, c):
            _row_copy(x_hbm, 0, buf.at[slot], r, sem.at[slot]).wait()
            return c

        lax.fori_loop(0, blk, drain, 0, unroll=DMA_LOOP_UNROLL)
        o_ref[...] = buf[slot].astype(o_ref.dtype)

    @pl.when(b >= n_used)
    def _():
        o_ref[...] = jnp.zeros_like(o_ref)


def _moe_gather(xn, row_tok, n_used, blk):
    n_rows = row_tok.shape[0]
    d = xn.shape[1]
    return pl.pallas_call(
        functools.partial(_moe_gather_kernel, blk),
        out_shape=jax.ShapeDtypeStruct((n_rows, d), BF16),
        grid_spec=pltpu.PrefetchScalarGridSpec(
            num_scalar_prefetch=2,
            grid=(n_rows // blk,),
            in_specs=[pl.BlockSpec(memory_space=pl.ANY)],
            out_specs=pl.BlockSpec((blk, d), lambda b, tok, nu: (b, 0)),
            scratch_shapes=[pltpu.VMEM((2, blk, d), F32), pltpu.SemaphoreType.DMA((2,))],
        ),
        compiler_params=_cparams("arbitrary"),
        name="moe_gather",
    )(row_tok, n_used, xn)


def _moe_up_kernel(be_ref, nused_ref, x_ref, wg_ref, wu_ref, wd32_ref, o_ref, wd16_ref):
    b = pl.program_id(0)
    wd16_ref[...] = wd32_ref[...].astype(wd16_ref.dtype)

    @pl.when(b < nused_ref[0])
    def _():
        x = x_ref[...]
        g = jnp.dot(x, wg_ref[0], preferred_element_type=F32)
        u = jnp.dot(x, wu_ref[0], preferred_element_type=F32)
        o_ref[...] = (g * jax.nn.sigmoid(g) * u).astype(o_ref.dtype)

    @pl.when(b >= nused_ref[0])
    def _():
        o_ref[...] = jnp.zeros_like(o_ref)


def _moe_down_kernel(be_ref, nused_ref, h_ref, wd_ref, o_ref):
    b = pl.program_id(0)

    @pl.when(b < nused_ref[0])
    def _():
        o_ref[...] = jnp.dot(h_ref[...], wd_ref[0], preferred_element_type=F32)

    @pl.when(b >= nused_ref[0])
    def _():
        o_ref[...] = jnp.zeros_like(o_ref)


def _moe_experts(x_rows, block_e, n_used, w_gate, w_up, w_down_f32, blk, *, tf=1024, tn=1024):
    n_rows, d = x_rows.shape
    n_e, _, f = w_gate.shape
    tf, tn = min(tf, f), min(tn, d)
    nf, nn = f // tf, d // tn
    n_blocks = n_rows // blk

    def wcol(nj):
        return lambda b, j, be, nu: (be[b], 0, jnp.where(b < nu[0], j, nj - 1))

    wd2 = w_down_f32.reshape(n_e * f, d)
    slab = _rider_slab(n_e * f, n_blocks * nf)
    rider = pl.BlockSpec((slab, d), functools.partial(
        lambda last, b, j, be, nu: (jnp.minimum(b * nf + j, last), 0), n_e * f // slab - 1))
    h_rows, w_down = pl.pallas_call(
        _moe_up_kernel,
        out_shape=[jax.ShapeDtypeStruct((n_rows, f), BF16), jax.ShapeDtypeStruct((n_e * f, d), BF16)],
        grid_spec=pltpu.PrefetchScalarGridSpec(
            num_scalar_prefetch=2,
            grid=(n_blocks, nf),
            in_specs=[pl.BlockSpec((blk, d), lambda b, j, be, nu: (b, 0)),
                      pl.BlockSpec((1, d, tf), wcol(nf)),
                      pl.BlockSpec((1, d, tf), wcol(nf)),
                      rider],
            out_specs=[pl.BlockSpec((blk, tf), lambda b, j, be, nu: (b, j)), rider],
        ),
        compiler_params=_cparams("arbitrary", "arbitrary"),
        name="moe_gate_up",
    )(block_e, n_used, x_rows, w_gate, w_up, wd2)
    w_down = w_down.reshape(n_e, f, d)
    y_rows = pl.pallas_call(
        _moe_down_kernel,
        out_shape=jax.ShapeDtypeStruct((n_rows, d), F32),
        grid_spec=pltpu.PrefetchScalarGridSpec(
            num_scalar_prefetch=2,
            grid=(n_rows // blk, nn),
            in_specs=[pl.BlockSpec((blk, f), lambda b, j, be, nu: (b, 0)),
                      pl.BlockSpec((1, f, tn), wcol(nn))],
            out_specs=pl.BlockSpec((blk, tn), lambda b, j, be, nu: (b, j)),
        ),
        compiler_params=_cparams("arbitrary", "arbitrary"),
        name="moe_down",
    )(block_e, n_used, h_rows, w_down)
    return y_rows


def _moe_combine_kernel(tt, pos_ref, x_ref, gate_ref, g_ref, y_hbm, o_ref, buf, sem):
    i = pl.program_id(0)
    n_tiles = pl.num_programs(0)

    def fetch(tile, slot):
        def issue(r, c):
            for s in range(TOP_K):
                _row_copy(y_hbm, pos_ref[(tile * tt + r) * TOP_K + s], buf.at[slot, s], r,
                          sem.at[slot]).start()
            return c

        lax.fori_loop(0, tt, issue, 0, unroll=DMA_LOOP_UNROLL)

    @pl.when(i == 0)
    def _():
        fetch(0, 0)

    @pl.when(i + 1 < n_tiles)
    def _():
        fetch(i + 1, (i + 1) % 2)

    slot = i % 2

    def drain(r, c):
        for s in range(TOP_K):
            _row_copy(y_hbm, 0, buf.at[slot, s], r, sem.at[slot]).wait()
        return c

    lax.fori_loop(0, tt, drain, 0, unroll=DMA_LOOP_UNROLL)
    gates = gate_ref[...]
    y = buf[slot, 0] * gates[:, 0:1] + buf[slot, 1] * gates[:, 1:2]
    o_ref[...] = _rms(x_ref[...] + y, g_ref[...])


def _moe_combine(x, y_rows, pos, gates, g, tt=256):
    m, d = x.shape
    tt = min(tt, m)
    row = lambda i, p: (i, 0)
    return pl.pallas_call(
        functools.partial(_moe_combine_kernel, tt),
        out_shape=jax.ShapeDtypeStruct((m, d), F32),
        grid_spec=pltpu.PrefetchScalarGridSpec(
            num_scalar_prefetch=1,
            grid=(m // tt,),
            in_specs=[pl.BlockSpec((tt, d), row),
                      pl.BlockSpec((tt, TOP_K), row),
                      pl.BlockSpec((1, d), lambda i, p: (0, 0)),
                      pl.BlockSpec(memory_space=pl.ANY)],
            out_specs=pl.BlockSpec((tt, d), row),
            scratch_shapes=[pltpu.VMEM((2, TOP_K, tt, d), F32), pltpu.SemaphoreType.DMA((2,))],
        ),
        compiler_params=_cparams("arbitrary"),
        name="moe_combine_norm",
    )(pos, x, gates, g.reshape(1, d), y_rows)


def _moe_routing(idx, n_experts, blk):
    n = idx.shape[0]
    flat_e = idx.reshape(-1)
    onehot = (flat_e[:, None] == jnp.arange(n_experts, dtype=jnp.int32)[None, :]).astype(jnp.int32)
    csum = jnp.cumsum(onehot, axis=0)
    rank = jnp.sum(csum * onehot, axis=1) - 1
    counts = csum[-1]
    padded = (counts + blk - 1) // blk * blk
    pend = jnp.cumsum(padded)
    pstart = pend - padded
    dest = (pstart[flat_e] + rank).astype(jnp.int32)
    n_rows = -(-(n * TOP_K + n_experts * (blk - 1)) // blk) * blk
    n_blocks = n_rows // blk
    flat_tok = jnp.arange(n * TOP_K, dtype=jnp.int32) // TOP_K
    row_tok = jnp.zeros((n_rows,), jnp.int32).at[dest].set(flat_tok)
    block_e = jnp.minimum(
        jnp.searchsorted(pend, jnp.arange(n_blocks, dtype=jnp.int32) * blk, side='right'),
        n_experts - 1).astype(jnp.int32)
    n_used = (pend[-1] // blk).astype(jnp.int32).reshape(1)
    return dest, row_tok, block_e, n_used


def _rope_tables(seq):
    inv_freq = ROPE_THETA ** (-jnp.arange(0, QK_ROPE, 2, dtype=F32) / QK_ROPE)
    ang = jnp.arange(seq, dtype=F32)[:, None] * inv_freq[None, :]
    cos, sin = jnp.cos(ang), jnp.sin(ang)
    pad = jnp.zeros((seq, LANES - QK_ROPE), F32)
    return (jnp.concatenate([cos, cos, pad], axis=1),
            jnp.concatenate([-sin, sin, pad], axis=1))


def kernel(x, a_norm, a_w_in, a_conv_w, a_conv_b, a_w_rgate, a_b_rgate, a_w_igate, a_b_igate, a_lambda, a_w_out, kv_norm, kv_w_down, kv_latent_norm, kv_w_rope, kv_w_up_k, kv_w_up_v, b_norm, b_w_q_down, b_q_latent_norm, b_w_q_up, b_w_q_rope, b_w_out, ffn_norm, ffn_w_gate, ffn_w_up, ffn_w_down, moe_norm, moe_w_router, moe_w_gate, moe_w_up, moe_w_down, final_norm):
    batch, seq, d = x.shape
    n = batch * seq
    n_heads = kv_w_up_k.shape[1]
    assert a_norm.shape[0] == 1 and b_norm.shape[0] == 1, "depth-2 block: one RG-LRU and one MLA layer"
    xf = x.reshape(n, d)
    tm = min(1024, seq)
    n_seq_tiles = seq // tm

    def mm_residual(a, w, res, name, riders=()):
        tn = min(512, w.shape[1])
        return _matmul(a, [w], [tn], _ep_residual,
                       [(res, (tm, tn), lambda i, j: (i, j))], [(w.shape[1], F32, tn)],
                       tm=tm, riders=riders, name=name)

    n_experts, _, f_exp = moe_w_gate.shape[1:]
    moe_gate32 = moe_w_gate[0].reshape(n_experts * d, f_exp)
    moe_up32 = moe_w_up[0].reshape(n_experts * d, f_exp)

    xn, = _norm(xf, [a_norm[0]])
    w_in = a_w_in[0].astype(BF16)
    tn = min(1024, w_in.shape[1])
    gr, w_out_a, wg, wu = _matmul(xn, [w_in], [tn], _ep_identity, [], [(w_in.shape[1], BF16, tn)],
                                  tm=tm, riders=[a_w_out[0], ffn_w_gate[0], ffn_w_up[0]], name="rglru_in")
    gated = _rglru(gr, a_conv_w[0], a_conv_b[0], a_w_rgate[0], a_b_rgate[0], a_w_igate[0],
                   a_b_igate[0], a_lambda[0], batch, seq)
    x1, wd = mm_residual(gated, w_out_a, xf, "rglru_out", riders=[ffn_w_down[0]])

    xn, = _norm(x1, [ffn_norm[0]])
    f = wg.shape[1]
    tf = min(512, f)
    hid, moe_gate16 = _matmul(xn, [wg, wu], [tf, tf], _ep_swiglu, [], [(f, BF16, tf)], tm=tm,
                              riders=[moe_gate32], name="ffn_gate_up")
    tn = min(1024, d)
    x2, moe_up16 = _matmul(hid, [wd], [tn], _ep_residual, [(x1, (tm, tn), lambda i, j: (i, j))],
                           [(d, F32, tn)], tm=tm, tk=min(2048, f), riders=[moe_up32], name="ffn_down")

    cos_t, sin_t = _rope_tables(seq)
    rope_specs = [(cos_t, (tm, LANES), lambda i, j: (i % n_seq_tiles, 0)),
                  (sin_t, (tm, LANES), lambda i, j: (i % n_seq_tiles, 0))]
    h_kv, xn_b = _norm(x2, [kv_norm, b_norm[0]])
    kv_rank = kv_w_down.shape[1]
    w_kv_rope = jnp.pad(kv_w_rope, ((0, 0), (0, LANES - QK_ROPE))).astype(BF16)
    c_kv, k_rope = _matmul(
        h_kv, [kv_w_down.astype(BF16), w_kv_rope], [kv_rank, LANES], _ep_kv_down,
        [(kv_latent_norm.reshape(1, kv_rank), (1, kv_rank), lambda i, j: (0, 0))] + rope_specs,
        [(kv_rank, BF16, kv_rank), (LANES, BF16, LANES)], tm=tm, name="kv_down")
    hk = n_heads * QK_NOPE
    tn = min(1024, hk)
    k_cat, v = _matmul(
        c_kv, [kv_w_up_k.reshape(kv_rank, hk).astype(BF16),
               kv_w_up_v.reshape(kv_rank, n_heads * V_DIM).astype(BF16)],
        [tn, tn], _ep_kv_up, [(k_rope, (tm, LANES), lambda i, j: (i, 0))],
        [(n_heads * Q_HEAD_STRIDE, BF16, tn // QK_NOPE * Q_HEAD_STRIDE), (n_heads * V_DIM, BF16, tn)],
        tm=tm, name="kv_up")

    q_rank = b_w_q_down.shape[2]
    c_q, = _matmul(xn_b, [b_w_q_down[0].astype(BF16)], [q_rank], _ep_rms,
                   [(b_q_latent_norm[0].reshape(1, q_rank), (1, q_rank), lambda i, j: (0, 0))],
                   [(q_rank, BF16, q_rank)], tm=tm, name="q_down")
    w_q_cat = jnp.concatenate(
        [b_w_q_up[0], b_w_q_rope[0],
         jnp.zeros((q_rank, n_heads, Q_HEAD_STRIDE - QK_NOPE - QK_ROPE), F32)],
        axis=-1).reshape(q_rank, n_heads * Q_HEAD_STRIDE).astype(BF16)
    tn = min(1024, n_heads * Q_HEAD_STRIDE)
    q_cat, = _matmul(c_q, [w_q_cat], [tn], _ep_q_up, rope_specs,
                     [(n_heads * Q_HEAD_STRIDE, BF16, tn)], tm=tm, name="q_up")
    o = _attention(q_cat, k_cat, v, batch, seq, n_heads)
    x3, = mm_residual(o, b_w_out[0].astype(BF16), x2, "attn_out")

    xn_moe, top_idx, gates = _moe_router(x3, moe_norm[0], moe_w_router[0])
    dest, row_tok, block_e, n_used = _moe_routing(top_idx, n_experts, MOE_BLOCK)
    x_rows = _moe_gather(xn_moe, row_tok, n_used, MOE_BLOCK)
    y_rows = _moe_experts(x_rows, block_e, n_used, moe_gate16.reshape(n_experts, d, f_exp),
                          moe_up16.reshape(n_experts, d, f_exp), moe_w_down[0], MOE_BLOCK)
    out = _moe_combine(x3, y_rows, dest, gates, final_norm)
    return out.reshape(batch, seq, d)
```

```python
import functools

import jax
import jax.numpy as jnp
from jax import lax
from jax.experimental import pallas as pl
from jax.experimental.pallas import tpu as pltpu

F32 = jnp.float32
BF16 = jnp.bfloat16

CHUNK = 64
LRU_BLOCK = 256
CONV_WIDTH = 4
LRU_C = 8.0
QK_NOPE = 128
QK_ROPE = 64
V_DIM = 128
ROPE_THETA = 10000.0
ATTN_SCALE = (QK_NOPE + QK_ROPE) ** -0.5
TOP_K = 2
EPS = 1e-6

VMEM_LIMIT_BYTES = 56 * 1024 * 1024
LANES = 128
SUBLANES = 8
BF16_SUBLANES = 16
Q_HEAD_STRIDE = 2 * LANES
V_HEAD_STRIDE = 2 * V_DIM
NEG_BIG = -1e30
LOG2_E = 1.4426950408889634

MOE_BLOCK = 512


def _cparams(*sem):
    return pltpu.CompilerParams(dimension_semantics=sem, vmem_limit_bytes=VMEM_LIMIT_BYTES)


def _rms(x, g):
    return x * lax.rsqrt(jnp.mean(x * x, axis=-1, keepdims=True) + EPS) * g


def _norm_kernel(n_out, x_ref, *refs):
    g_refs, o_refs = refs[:n_out], refs[n_out:]
    x = x_ref[...]
    y = x * lax.rsqrt(jnp.mean(x * x, axis=-1, keepdims=True) + EPS)
    for g_ref, o_ref in zip(g_refs, o_refs):
        o_ref[...] = (y * g_ref[...]).astype(o_ref.dtype)


def _norm(x, gains, out_dtype=BF16, tm=256):
    m, d = x.shape
    tm = min(tm, m)
    n = len(gains)
    row = pl.BlockSpec((tm, d), lambda i: (i, 0))
    vec = pl.BlockSpec((1, d), lambda i: (0, 0))
    outs = pl.pallas_call(
        functools.partial(_norm_kernel, n),
        out_shape=[jax.ShapeDtypeStruct((m, d), out_dtype)] * n,
        grid=(m // tm,),
        in_specs=[row] + [vec] * n,
        out_specs=[row] * n,
        compiler_params=_cparams("parallel"),
        name="rmsnorm",
    )(x, *[g.reshape(1, d) for g in gains])
    return outs


def _mm_kernel(nw, ne, nr, no, nk, epilogue, a_ref, *refs):
    w_refs = refs[:nw]
    e_refs = refs[nw:nw + ne]
    rin_refs = refs[nw + ne:nw + ne + nr]
    o_refs = refs[nw + ne + nr:nw + ne + nr + no]
    rout_refs = refs[nw + ne + nr + no:nw + ne + 2 * nr + no]
    acc_refs = refs[nw + ne + 2 * nr + no:]

    for rin, rout in zip(rin_refs, rout_refs):
        rout[...] = rin[...].astype(rout.dtype)

    def finish(accs):
        vals = epilogue(accs, e_refs)
        for o_ref, v in zip(o_refs, vals):
            o_ref[...] = v.astype(o_ref.dtype)

    if nk == 1:
        finish([jnp.dot(a_ref[...], w[...], preferred_element_type=F32) for w in w_refs])
        return

    k = pl.program_id(2)

    @pl.when(k == 0)
    def _():
        for acc in acc_refs:
            acc[...] = jnp.zeros_like(acc)

    for acc, w in zip(acc_refs, w_refs):
        acc[...] += jnp.dot(a_ref[...], w[...], preferred_element_type=F32)

    @pl.when(k == nk - 1)
    def _():
        finish([acc[...] for acc in acc_refs])


def _rider_slab(rows, n_steps):
    slab = -(-rows // n_steps)
    slab = -(-slab // BF16_SUBLANES) * BF16_SUBLANES
    while rows % slab:
        slab += BF16_SUBLANES
    return slab


def _matmul(a, ws, tns, epilogue, extras, outs, *, tm, tk=None, riders=(), name):
    m, kdim = a.shape
    tm = min(tm, m)
    tk = kdim if tk is None else min(tk, kdim)
    nk = kdim // tk
    nj = ws[0].shape[1] // tns[0]
    n_steps = (m // tm) * nj * nk
    in_specs = [pl.BlockSpec((tm, tk), lambda i, j, k: (i, k))]
    for w, tn in zip(ws, tns):
        assert w.shape[1] // tn == nj
        in_specs.append(pl.BlockSpec((tk, tn), lambda i, j, k: (k, j)))
    for _, bs, im in extras:
        in_specs.append(pl.BlockSpec(bs, functools.partial(lambda im, i, j, k: im(i, j), im)))
    out_shape = [jax.ShapeDtypeStruct((m, n), dt) for n, dt, _ in outs]
    out_specs = [pl.BlockSpec((tm, tn), lambda i, j, k: (i, j)) for _, _, tn in outs]
    rider_specs = []
    for r in riders:
        rows, cols = r.shape
        slab = _rider_slab(rows, n_steps)
        rider_specs.append(pl.BlockSpec(
            (slab, cols),
            functools.partial(lambda last, i, j, k: (jnp.minimum((i * nj + j) * nk + k, last), 0),
                              rows // slab - 1)))
        out_shape.append(jax.ShapeDtypeStruct((rows, cols), BF16))
    scratch = [pltpu.VMEM((tm, tn), F32) for tn in tns] if nk > 1 else []
    semantics = ("arbitrary",) * 3 if riders else ("parallel", "parallel", "arbitrary")
    res = pl.pallas_call(
        functools.partial(_mm_kernel, len(ws), len(extras), len(riders), len(outs), nk, epilogue),
        out_shape=out_shape,
        grid=(m // tm, nj, nk),
        in_specs=in_specs + rider_specs,
        out_specs=out_specs + rider_specs,
        scratch_shapes=scratch,
        compiler_params=_cparams(*semantics),
        name=name,
    )(a, *ws, *[e[0] for e in extras], *riders)
    return res


def _ep_identity(accs, e_refs):
    return accs


def _ep_residual(accs, e_refs):
    return [e_refs[0][...] + accs[0]]


def _ep_swiglu(accs, e_refs):
    g, u = accs
    return [g * jax.nn.sigmoid(g) * u]


def _ep_rms(accs, e_refs):
    return [_rms(accs[0], e_refs[0][...])]


def _rope_slab(x, cos_t, sin_t):
    lane = lax.broadcasted_iota(jnp.int32, x.shape, 1)
    swapped = jnp.where(lane < QK_ROPE // 2,
                        pltpu.roll(x, LANES - QK_ROPE // 2, 1),
                        pltpu.roll(x, QK_ROPE // 2, 1))
    return x * cos_t + swapped * sin_t


def _ep_kv_down(accs, e_refs):
    g_ref, cos_ref, sin_ref = e_refs
    return [_rms(accs[0], g_ref[...]), _rope_slab(accs[1], cos_ref[...], sin_ref[...])]


def _ep_q_up(accs, e_refs):
    cos_ref, sin_ref = e_refs
    acc = accs[0]
    cos_t, sin_t = cos_ref[...], sin_ref[...]
    cols = []
    for h in range(acc.shape[1] // Q_HEAD_STRIDE):
        base = h * Q_HEAD_STRIDE
        cols.append(acc[:, base:base + LANES])
        cols.append(_rope_slab(acc[:, base + LANES:base + Q_HEAD_STRIDE], cos_t, sin_t))
    return [jnp.concatenate(cols, axis=1) * (ATTN_SCALE * LOG2_E)]


def _ep_kv_up(accs, e_refs):
    acc_k, acc_v = accs
    k_rope = e_refs[0][...].astype(F32)
    ones = jnp.ones((acc_v.shape[0], V_HEAD_STRIDE - V_DIM), F32)
    k_cols, v_cols = [], []
    for h in range(acc_k.shape[1] // QK_NOPE):
        k_cols += [acc_k[:, h * QK_NOPE:(h + 1) * QK_NOPE], k_rope]
        v_cols += [acc_v[:, h * V_DIM:(h + 1) * V_DIM], ones]
    return [jnp.concatenate(k_cols, axis=1), jnp.concatenate(v_cols, axis=1)]


def _rglru_kernel(ts, tc, gate_ref, rec_ref, cw_ref, cb_ref, wr_ref, br_ref, wi_ref, bi_ref,
                  lam_ref, ride32_ref, o_ref, ride16_ref, tail_ref, h_ref, a_scr, x_scr):
    t = pl.program_id(2)
    ride16_ref[...] = ride32_ref[...].astype(ride16_ref.dtype)

    @pl.when(t == 0)
    def _():
        tail_ref[...] = jnp.zeros_like(tail_ref)
        h_ref[...] = jnp.zeros_like(h_ref)

    x = rec_ref[0].astype(F32)
    tail = tail_ref[...]
    cw = cw_ref[...]
    row8 = lax.broadcasted_iota(jnp.int32, (SUBLANES, tc), 0)
    u = x * cw[CONV_WIDTH - 1:CONV_WIDTH, :] + cb_ref[...]
    for k in range(1, CONV_WIDTH):
        xs = pltpu.roll(x, k, 0)
        top = jnp.where(row8 < k, pltpu.roll(tail, k, 0), xs[0:SUBLANES])
        xs = jnp.concatenate([top, xs[SUBLANES:]], axis=0)
        u = u + xs * cw[CONV_WIDTH - 1 - k:CONV_WIDTH - k, :]
    tail_ref[...] = x[ts - SUBLANES:ts]

    ub = u.astype(BF16)
    r_parts, i_parts = [], []
    for g in range(tc // LRU_BLOCK):
        ug = ub[:, g * LRU_BLOCK:(g + 1) * LRU_BLOCK]
        r_parts.append(jnp.dot(ug, wr_ref[g], preferred_element_type=F32))
        i_parts.append(jnp.dot(ug, wi_ref[g], preferred_element_type=F32))
    r = jax.nn.sigmoid(jnp.concatenate(r_parts, axis=1) + br_ref[...])
    ig = jax.nn.sigmoid(jnp.concatenate(i_parts, axis=1) + bi_ref[...])

    neg_lam = -lam_ref[...]
    softplus = jnp.maximum(neg_lam, 0.0) + jnp.log1p(jnp.exp(-jnp.abs(neg_lam)))
    log_a = (-LRU_C) * r * softplus
    a = jnp.exp(log_a)
    z = -jnp.tanh(log_a) * (1.0 + a * a)
    xin = (ig * u) * jnp.where(z > 0.0, z * lax.rsqrt(z), 0.0)

    nb = ts // SUBLANES
    grow = lax.broadcasted_iota(jnp.int32, (nb, LANES), 0)
    gate = jax.nn.gelu(gate_ref[0].astype(F32), approximate=True)
    for c in range(tc // LANES):
        lanes = slice(c * LANES, (c + 1) * LANES)
        a_scr[c] = a[:, lanes]
        x_scr[c] = xin[:, lanes]
        decay, state = [], []
        for r in range(SUBLANES):
            a_r = a_scr[c, pl.ds(r, nb, stride=SUBLANES), :]
            x_r = x_scr[c, pl.ds(r, nb, stride=SUBLANES), :]
            decay.append(a_r if r == 0 else a_r * decay[-1])
            state.append(x_r if r == 0 else a_r * state[-1] + x_r)
        ca, cx = decay[-1], state[-1]
        shift = 1
        while shift < nb:
            if shift < SUBLANES:
                keep = grow >= shift
                a_s = jnp.where(keep, pltpu.roll(ca, shift, 0), 1.0)
                x_s = jnp.where(keep, pltpu.roll(cx, shift, 0), 0.0)
            else:
                a_s = jnp.concatenate([jnp.ones((shift, LANES), F32), ca[:nb - shift]], axis=0)
                x_s = jnp.concatenate([jnp.zeros((shift, LANES), F32), cx[:nb - shift]], axis=0)
            cx = ca * x_s + cx
            ca = ca * a_s
            shift *= 2
        h_prev = h_ref[0:1, lanes]
        group_out = ca * h_prev + cx
        group_in = jnp.where(grow == 0, h_prev, pltpu.roll(group_out, 1, 0))
        for r in range(SUBLANES):
            x_scr[c, pl.ds(r, nb, stride=SUBLANES), :] = decay[r] * group_in + state[r]
        h_ref[:, lanes] = jnp.broadcast_to(group_out[nb - 1:nb, :], (SUBLANES, LANES))
        o_ref[0, :, lanes] = (gate[:, lanes] * x_scr[c]).astype(o_ref.dtype)


def _rglru(gr, conv_w, conv_b, w_r, b_r, w_i, b_i, lam, batch, seq, rider32, *, ts=512, tc=512):
    d = conv_w.shape[1]
    ts, tc = min(ts, seq), min(tc, d)
    nc, nt = d // tc, seq // ts
    gpb = tc // LRU_BLOCK
    gr3 = gr.reshape(batch, seq, 2 * d)
    vec = pl.BlockSpec((1, tc), lambda b, c, t: (0, c))
    wspec = pl.BlockSpec((gpb, LRU_BLOCK, LRU_BLOCK), lambda b, c, t: (c, 0, 0))
    r_rows, r_cols = rider32.shape
    slab = _rider_slab(r_rows, batch * nc * nt)
    rider = pl.BlockSpec((slab, r_cols), functools.partial(
        lambda last, b, c, t: (jnp.minimum((b * nc + c) * nt + t, last), 0), r_rows // slab - 1))
    out, rider16 = pl.pallas_call(
        functools.partial(_rglru_kernel, ts, tc),
        out_shape=[jax.ShapeDtypeStruct((batch, seq, d), BF16),
                   jax.ShapeDtypeStruct((r_rows, r_cols), BF16)],
        grid=(batch, nc, nt),
        in_specs=[
            pl.BlockSpec((1, ts, tc), lambda b, c, t: (b, t, c)),
            pl.BlockSpec((1, ts, tc), lambda b, c, t: (b, t, nc + c)),
            pl.BlockSpec((CONV_WIDTH, tc), lambda b, c, t: (0, c)),
            vec, wspec, vec, wspec, vec, vec, rider,
        ],
        out_specs=[pl.BlockSpec((1, ts, tc), lambda b, c, t: (b, t, c)), rider],
        scratch_shapes=[pltpu.VMEM((SUBLANES, tc), F32), pltpu.VMEM((SUBLANES, tc), F32),
                        pltpu.VMEM((tc // LANES, ts, LANES), F32),
                        pltpu.VMEM((tc // LANES, ts, LANES), F32)],
        compiler_params=_cparams("arbitrary", "arbitrary", "arbitrary"),
        name="rglru",
    )(gr3, gr3, conv_w, conv_b.reshape(1, d), w_r.astype(BF16), b_r.reshape(1, d),
      w_i.astype(BF16), b_i.reshape(1, d), lam.reshape(1, d), rider32)
    return out.reshape(batch * seq, d), rider16


def _attn_kernel(tq, hps, q_ref, k_ref, v_ref, o_ref, sa_ref, sb_ref, m_ref, acc_ref):
    qi = pl.program_id(2)
    m_ref[...] = jnp.full_like(m_ref, NEG_BIG)
    acc_ref[...] = jnp.zeros_like(acc_ref)
    contract_last = (((1,), (1,)), ((), ()))

    def scores(j, s_ref):
        k0 = pl.multiple_of(j * tq, tq)
        for h in range(hps):
            q = q_ref[0, :, h * Q_HEAD_STRIDE:(h + 1) * Q_HEAD_STRIDE]
            k = k_ref[0, pl.ds(k0, tq), h * Q_HEAD_STRIDE:(h + 1) * Q_HEAD_STRIDE]
            s_ref[h] = lax.dot_general(q, k, contract_last, preferred_element_type=F32)

    def consume(j, s_ref, masked):
        k0 = pl.multiple_of(j * tq, tq)
        for h in range(hps):
            s = s_ref[h]
            if masked:
                qc = lax.broadcasted_iota(jnp.int32, s.shape, 0) // CHUNK
                kc = lax.broadcasted_iota(jnp.int32, s.shape, 1) // CHUNK
                s = jnp.where(qc >= kc, s, NEG_BIG)
            m_prev = m_ref[h]
            m_next = jnp.maximum(m_prev, jnp.max(s, axis=-1, keepdims=True))
            p = jnp.exp2(s - jnp.tile(m_next, (1, tq // LANES)))
            alpha = jnp.exp2(m_prev - m_next)
            m_ref[h] = m_next
            v = v_ref[0, pl.ds(k0, tq), h * V_HEAD_STRIDE:(h + 1) * V_HEAD_STRIDE]
            acc_ref[h] = (jnp.tile(alpha, (1, V_HEAD_STRIDE // LANES)) * acc_ref[h]
                          + jnp.dot(p.astype(BF16), v, preferred_element_type=F32))

    scores(0, sa_ref)

    def pair(t, carry):
        scores(2 * t + 1, sb_ref)
        consume(2 * t, sa_ref, False)
        scores(2 * t + 2, sa_ref)
        consume(2 * t + 1, sb_ref, False)
        return carry

    lax.fori_loop(0, qi // 2, pair, 0)

    @pl.when(qi % 2 == 1)
    def _():
        scores(qi, sb_ref)
        consume(qi - 1, sa_ref, False)
        consume(qi, sb_ref, True)

    @pl.when(qi % 2 == 0)
    def _():
        consume(qi, sa_ref, True)

    for h in range(hps):
        acc = acc_ref[h]
        o_ref[0, :, h * V_DIM:(h + 1) * V_DIM] = (acc[:, :V_DIM] / acc[:, V_DIM:]).astype(o_ref.dtype)


def _attention(q_cat, k_cat, v_ones, batch, seq, n_heads, *, tq=512, hps=2):
    tq = min(tq, seq)
    q3 = q_cat.reshape(batch, seq, n_heads * Q_HEAD_STRIDE)
    k3 = k_cat.reshape(batch, seq, n_heads * Q_HEAD_STRIDE)
    v3 = v_ones.reshape(batch, seq, n_heads * V_HEAD_STRIDE)
    out = pl.pallas_call(
        functools.partial(_attn_kernel, tq, hps),
        out_shape=jax.ShapeDtypeStruct((batch, seq, n_heads * V_DIM), BF16),
        grid=(batch, n_heads // hps, seq // tq),
        in_specs=[
            pl.BlockSpec((1, tq, hps * Q_HEAD_STRIDE), lambda b, h, i: (b, i, h)),
            pl.BlockSpec((1, seq, hps * Q_HEAD_STRIDE), lambda b, h, i: (b, 0, h)),
            pl.BlockSpec((1, seq, hps * V_HEAD_STRIDE), lambda b, h, i: (b, 0, h)),
        ],
        out_specs=pl.BlockSpec((1, tq, hps * V_DIM), lambda b, h, i: (b, i, h)),
        scratch_shapes=[pltpu.VMEM((hps, tq, tq), F32), pltpu.VMEM((hps, tq, tq), F32),
                        pltpu.VMEM((hps, tq, LANES), F32),
                        pltpu.VMEM((hps, tq, V_HEAD_STRIDE), F32)],
        compiler_params=_cparams("parallel", "parallel", "arbitrary"),
        name="attention",
    )(q3, k3, v3)
    return out.reshape(batch * seq, n_heads * V_DIM)


def _moe_router_kernel(x_ref, g_ref, wh_ref, wl_ref, xn_ref, idx_ref, gate_ref):
    x = x_ref[...]
    xn = _rms(x, g_ref[...])
    xn_ref[...] = xn
    xh = xn.astype(BF16)
    xl = (xn - xh.astype(F32)).astype(BF16)
    wh, wl = wh_ref[...], wl_ref[...]
    logits = (jnp.dot(xh, wh, preferred_element_type=F32) + jnp.dot(xh, wl, preferred_element_type=F32)
              + jnp.dot(xl, wh, preferred_element_type=F32))
    n_e = logits.shape[1]
    lane = lax.broadcasted_iota(jnp.int32, logits.shape, 1)
    m1 = jnp.max(logits, axis=-1, keepdims=True)
    i1 = jnp.min(jnp.where(logits == m1, lane, n_e), axis=-1, keepdims=True)
    rest = jnp.where(lane == i1, -jnp.inf, logits)
    m2 = jnp.max(rest, axis=-1, keepdims=True)
    i2 = jnp.min(jnp.where(rest == m2, lane, n_e), axis=-1, keepdims=True)
    e2 = jnp.exp(m2 - m1)
    denom = 1.0 + e2
    slot = lax.broadcasted_iota(jnp.int32, idx_ref.shape, 1)
    idx_ref[...] = jnp.where(slot == 0, i1, i2)
    gate_ref[...] = jnp.where(slot == 0, 1.0 / denom, e2 / denom)


def _moe_router(x, g, w_router, tm=256):
    m, d = x.shape
    tm = min(tm, m)
    n_e = w_router.shape[1]
    w_hi = w_router.astype(BF16)
    row = pl.BlockSpec((tm, d), lambda i: (i, 0))
    small = pl.BlockSpec((tm, TOP_K), lambda i: (i, 0))
    return pl.pallas_call(
        _moe_router_kernel,
        out_shape=[jax.ShapeDtypeStruct((m, d), F32),
                   jax.ShapeDtypeStruct((m, TOP_K), jnp.int32),
                   jax.ShapeDtypeStruct((m, TOP_K), F32)],
        grid=(m // tm,),
        in_specs=[row, pl.BlockSpec((1, d), lambda i: (0, 0)),
                  pl.BlockSpec((d, n_e), lambda i: (0, 0)),
                  pl.BlockSpec((d, n_e), lambda i: (0, 0))],
        out_specs=[row, small, small],
        compiler_params=_cparams("parallel"),
        name="moe_norm_router",
    )(x, g.reshape(1, d), w_hi, (w_router - w_hi.astype(F32)).astype(BF16))


def _row_copy(src_hbm, src_row, buf, dst_row, sem):
    return pltpu.make_async_copy(src_hbm.at[pl.ds(src_row, 1)], buf.at[pl.ds(dst_row, 1)], sem)


DMA_LOOP_UNROLL = 8


def _moe_gather_kernel(blk, tok_ref, nused_ref, x_hbm, o_ref, buf, sem):
    b = pl.program_id(0)
    n_used = nused_ref[0]

    def fetch(block, slot):
        def issue(r, c):
            _row_copy(x_hbm, tok_ref[block * blk + r], buf.at[slot], r, sem.at[slot]).start()
            return c

        lax.fori_loop(0, blk, issue, 0, unroll=DMA_LOOP_UNROLL)

    @pl.when((b == 0) & (n_used > 0))
    def _():
        fetch(0, 0)

    @pl.when(b + 1 < n_used)
    def _():
        fetch(b + 1, (b + 1) % 2)

    @pl.when(b < n_used)
    def _():
        slot = b % 2

        def drain(r, c):
            _row_copy(x_hbm, 0, buf.at[slot], r, sem.at[slot]).wait()
            return c

        lax.fori_loop(0, blk, drain, 0, unroll=DMA_LOOP_UNROLL)
        o_ref[...] = buf[slot].astype(o_ref.dtype)

    @pl.when(b >= n_used)
    def _():
        o_ref[...] = jnp.zeros_like(o_ref)


def _moe_gather(xn, row_tok, n_used, blk):
    n_rows = row_tok.shape[0]
    d = xn.shape[1]
    return pl.pallas_call(
        functools.partial(_moe_gather_kernel, blk),
        out_shape=jax.ShapeDtypeStruct((n_rows, d), BF16),
        grid_spec=pltpu.PrefetchScalarGridSpec(
            num_scalar_prefetch=2,
            grid=(n_rows // blk,),
            in_specs=[pl.BlockSpec(memory_space=pl.ANY)],
            out_specs=pl.BlockSpec((blk, d), lambda b, tok, nu: (b, 0)),
            scratch_shapes=[pltpu.VMEM((2, blk, d), F32), pltpu.SemaphoreType.DMA((2,))],
        ),
        compiler_params=_cparams("arbitrary"),
        name="moe_gather",
    )(row_tok, n_used, xn)


def _for_filled_rows(blk, n_valid, compute, o_ref):
    half = blk // 2

    @pl.when(n_valid > half)
    def _():
        o_ref[...] = compute(slice(0, blk)).astype(o_ref.dtype)

    @pl.when((n_valid > 0) & (n_valid <= half))
    def _():
        o_ref[0:half, :] = compute(slice(0, half)).astype(o_ref.dtype)
        o_ref[half:blk, :] = jnp.zeros((blk - half, o_ref.shape[1]), o_ref.dtype)

    @pl.when(n_valid == 0)
    def _():
        o_ref[...] = jnp.zeros_like(o_ref)


def _moe_up_kernel(blk, be_ref, valid_ref, x_ref, wg_ref, wu_ref, wd32_ref, o_ref, wd16_ref):
    wd16_ref[...] = wd32_ref[...].astype(wd16_ref.dtype)

    def compute(rows):
        x = x_ref[rows, :]
        g = jnp.dot(x, wg_ref[0], preferred_element_type=F32)
        u = jnp.dot(x, wu_ref[0], preferred_element_type=F32)
        return g * jax.nn.sigmoid(g) * u

    _for_filled_rows(blk, valid_ref[pl.program_id(0)], compute, o_ref)


def _moe_down_kernel(blk, be_ref, valid_ref, h_ref, wd_ref, o_ref):
    def compute(rows):
        return jnp.dot(h_ref[rows, :], wd_ref[0], preferred_element_type=F32)

    _for_filled_rows(blk, valid_ref[pl.program_id(0)], compute, o_ref)


def _moe_experts(x_rows, block_e, block_valid, w_gate, w_up, w_down_f32, blk, *, tf=1024, tn=1024):
    n_rows, d = x_rows.shape
    n_e, _, f = w_gate.shape
    tf, tn = min(tf, f), min(tn, d)
    nf, nn = f // tf, d // tn
    n_blocks = n_rows // blk

    def wcol(nj):
        return lambda b, j, be, nv: (be[b], 0, jnp.where(nv[b] > 0, j, nj - 1))

    wd2 = w_down_f32.reshape(n_e * f, d)
    slab = _rider_slab(n_e * f, n_blocks * nf)
    rider = pl.BlockSpec((slab, d), functools.partial(
        lambda last, b, j, be, nu: (jnp.minimum(b * nf + j, last), 0), n_e * f // slab - 1))
    h_rows, w_down = pl.pallas_call(
        functools.partial(_moe_up_kernel, blk),
        out_shape=[jax.ShapeDtypeStruct((n_rows, f), BF16), jax.ShapeDtypeStruct((n_e * f, d), BF16)],
        grid_spec=pltpu.PrefetchScalarGridSpec(
            num_scalar_prefetch=2,
            grid=(n_blocks, nf),
            in_specs=[pl.BlockSpec((blk, d), lambda b, j, be, nu: (b, 0)),
                      pl.BlockSpec((1, d, tf), wcol(nf)),
                      pl.BlockSpec((1, d, tf), wcol(nf)),
                      rider],
            out_specs=[pl.BlockSpec((blk, tf), lambda b, j, be, nu: (b, j)), rider],
        ),
        compiler_params=_cparams("arbitrary", "arbitrary"),
        name="moe_gate_up",
    )(block_e, block_valid, x_rows, w_gate, w_up, wd2)
    w_down = w_down.reshape(n_e, f, d)
    y_rows = pl.pallas_call(
        functools.partial(_moe_down_kernel, blk),
        out_shape=jax.ShapeDtypeStruct((n_rows, d), F32),
        grid_spec=pltpu.PrefetchScalarGridSpec(
            num_scalar_prefetch=2,
            grid=(n_rows // blk, nn),
            in_specs=[pl.BlockSpec((blk, f), lambda b, j, be, nu: (b, 0)),
                      pl.BlockSpec((1, f, tn), wcol(nn))],
            out_specs=pl.BlockSpec((blk, tn), lambda b, j, be, nu: (b, j)),
        ),
        compiler_params=_cparams("arbitrary", "arbitrary"),
        name="moe_down",
    )(block_e, block_valid, h_rows, w_down)
    return y_rows


def _moe_combine_kernel(tt, pos_ref, x_ref, gate_ref, g_ref, y_hbm, o_ref, buf, sem):
    i = pl.program_id(0)
    n_tiles = pl.num_programs(0)

    def fetch(tile, slot):
        def issue(r, c):
            for s in range(TOP_K):
                _row_copy(y_hbm, pos_ref[(tile * tt + r) * TOP_K + s], buf.at[slot, s], r,
                          sem.at[slot]).start()
            return c

        lax.fori_loop(0, tt, issue, 0, unroll=DMA_LOOP_UNROLL)

    @pl.when(i == 0)
    def _():
        fetch(0, 0)

    @pl.when(i + 1 < n_tiles)
    def _():
        fetch(i + 1, (i + 1) % 2)

    slot = i % 2

    def drain(r, c):
        for s in range(TOP_K):
            _row_copy(y_hbm, 0, buf.at[slot, s], r, sem.at[slot]).wait()
        return c

    lax.fori_loop(0, tt, drain, 0, unroll=DMA_LOOP_UNROLL)
    gates = gate_ref[...]
    y = buf[slot, 0] * gates[:, 0:1] + buf[slot, 1] * gates[:, 1:2]
    o_ref[...] = _rms(x_ref[...] + y, g_ref[...])


def _moe_combine(x, y_rows, pos, gates, g, tt=256):
    m, d = x.shape
    tt = min(tt, m)
    row = lambda i, p: (i, 0)
    return pl.pallas_call(
        functools.partial(_moe_combine_kernel, tt),
        out_shape=jax.ShapeDtypeStruct((m, d), F32),
        grid_spec=pltpu.PrefetchScalarGridSpec(
            num_scalar_prefetch=1,
            grid=(m // tt,),
            in_specs=[pl.BlockSpec((tt, d), row),
                      pl.BlockSpec((tt, TOP_K), row),
                      pl.BlockSpec((1, d), lambda i, p: (0, 0)),
                      pl.BlockSpec(memory_space=pl.ANY)],
            out_specs=pl.BlockSpec((tt, d), row),
            scratch_shapes=[pltpu.VMEM((2, TOP_K, tt, d), F32), pltpu.SemaphoreType.DMA((2,))],
        ),
        compiler_params=_cparams("arbitrary"),
        name="moe_combine_norm",
    )(pos, x, gates, g.reshape(1, d), y_rows)


def _moe_routing(idx, n_experts, blk):
    n = idx.shape[0]
    flat_e = idx.reshape(-1)
    onehot = (flat_e[:, None] == jnp.arange(n_experts, dtype=jnp.int32)[None, :]).astype(jnp.int32)
    csum = jnp.cumsum(onehot, axis=0)
    rank = jnp.sum(csum * onehot, axis=1) - 1
    counts = csum[-1]
    padded = (counts + blk - 1) // blk * blk
    pend = jnp.cumsum(padded)
    pstart = pend - padded
    dest = (pstart[flat_e] + rank).astype(jnp.int32)
    n_rows = -(-(n * TOP_K + n_experts * (blk - 1)) // blk) * blk
    n_blocks = n_rows // blk
    flat_tok = jnp.arange(n * TOP_K, dtype=jnp.int32) // TOP_K
    row_tok = jnp.zeros((n_rows,), jnp.int32).at[dest].set(flat_tok)
    block_e = jnp.minimum(
        jnp.searchsorted(pend, jnp.arange(n_blocks, dtype=jnp.int32) * blk, side='right'),
        n_experts - 1).astype(jnp.int32)
    n_used = (pend[-1] // blk).astype(jnp.int32).reshape(1)
    block_start = jnp.arange(n_blocks, dtype=jnp.int32) * blk
    block_valid = jnp.clip(pstart[block_e] + counts[block_e] - block_start, 0, blk)
    block_valid = jnp.where(block_start < pend[-1], block_valid, 0).astype(jnp.int32)
    return dest, row_tok, block_e, block_valid, n_used


def _rope_tables(seq):
    inv_freq = ROPE_THETA ** (-jnp.arange(0, QK_ROPE, 2, dtype=F32) / QK_ROPE)
    ang = jnp.arange(seq, dtype=F32)[:, None] * inv_freq[None, :]
    cos, sin = jnp.cos(ang), jnp.sin(ang)
    pad = jnp.zeros((seq, LANES - QK_ROPE), F32)
    return (jnp.concatenate([cos, cos, pad], axis=1),
            jnp.concatenate([-sin, sin, pad], axis=1))


def kernel(x, a_norm, a_w_in, a_conv_w, a_conv_b, a_w_rgate, a_b_rgate, a_w_igate, a_b_igate, a_lambda, a_w_out, kv_norm, kv_w_down, kv_latent_norm, kv_w_rope, kv_w_up_k, kv_w_up_v, b_norm, b_w_q_down, b_q_latent_norm, b_w_q_up, b_w_q_rope, b_w_out, ffn_norm, ffn_w_gate, ffn_w_up, ffn_w_down, moe_norm, moe_w_router, moe_w_gate, moe_w_up, moe_w_down, final_norm):
    batch, seq, d = x.shape
    n = batch * seq
    n_heads = kv_w_up_k.shape[1]
    assert a_norm.shape[0] == 1 and b_norm.shape[0] == 1, "depth-2 block: one RG-LRU and one MLA layer"
    xf = x.reshape(n, d)
    tm = min(1024, seq)
    n_seq_tiles = seq // tm

    def mm_residual(a, w, res, name, riders=()):
        tn = min(512, w.shape[1])
        return _matmul(a, [w], [tn], _ep_residual,
                       [(res, (tm, tn), lambda i, j: (i, j))], [(w.shape[1], F32, tn)],
                       tm=tm, riders=riders, name=name)

    n_experts, _, f_exp = moe_w_gate.shape[1:]
    moe_gate32 = moe_w_gate[0].reshape(n_experts * d, f_exp)
    moe_up32 = moe_w_up[0].reshape(n_experts * d, f_exp)

    xn, = _norm(xf, [a_norm[0]])
    w_in = a_w_in[0].astype(BF16)
    tn = min(1024, w_in.shape[1])
    gr, w_out_a, wg, wu = _matmul(xn, [w_in], [tn], _ep_identity, [], [(w_in.shape[1], BF16, tn)],
                                  tm=tm, riders=[a_w_out[0], ffn_w_gate[0], ffn_w_up[0]], name="rglru_in")
    gated, moe_up16 = _rglru(gr, a_conv_w[0], a_conv_b[0], a_w_rgate[0], a_b_rgate[0], a_w_igate[0],
                             a_b_igate[0], a_lambda[0], batch, seq, moe_up32)
    x1, wd = mm_residual(gated, w_out_a, xf, "rglru_out", riders=[ffn_w_down[0]])

    xn, = _norm(x1, [ffn_norm[0]])
    f = wg.shape[1]
    tf = min(512, f)
    hid, moe_gate16 = _matmul(xn, [wg, wu], [tf, tf], _ep_swiglu, [], [(f, BF16, tf)], tm=tm,
                              riders=[moe_gate32], name="ffn_gate_up")
    tn = min(1024, d)
    x2, = _matmul(hid, [wd], [tn], _ep_residual, [(x1, (tm, tn), lambda i, j: (i, j))],
                  [(d, F32, tn)], tm=tm, tk=min(2048, f), name="ffn_down")

    cos_t, sin_t = _rope_tables(seq)
    rope_specs = [(cos_t, (tm, LANES), lambda i, j: (i % n_seq_tiles, 0)),
                  (sin_t, (tm, LANES), lambda i, j: (i % n_seq_tiles, 0))]
    h_kv, xn_b = _norm(x2, [kv_norm, b_norm[0]])
    kv_rank = kv_w_down.shape[1]
    w_kv_rope = jnp.pad(kv_w_rope, ((0, 0), (0, LANES - QK_ROPE))).astype(BF16)
    c_kv, k_rope = _matmul(
        h_kv, [kv_w_down.astype(BF16), w_kv_rope], [kv_rank, LANES], _ep_kv_down,
        [(kv_latent_norm.reshape(1, kv_rank), (1, kv_rank), lambda i, j: (0, 0))] + rope_specs,
        [(kv_rank, BF16, kv_rank), (LANES, BF16, LANES)], tm=tm, name="kv_down")
    hk = n_heads * QK_NOPE
    tn = min(1024, hk)
    k_cat, v = _matmul(
        c_kv, [kv_w_up_k.reshape(kv_rank, hk).astype(BF16),
               kv_w_up_v.reshape(kv_rank, n_heads * V_DIM).astype(BF16)],
        [tn, tn], _ep_kv_up, [(k_rope, (tm, LANES), lambda i, j: (i, 0))],
        [(n_heads * Q_HEAD_STRIDE, BF16, tn // QK_NOPE * Q_HEAD_STRIDE),
         (n_heads * V_HEAD_STRIDE, BF16, tn // V_DIM * V_HEAD_STRIDE)],
        tm=tm, name="kv_up")

    q_rank = b_w_q_down.shape[2]
    c_q, = _matmul(xn_b, [b_w_q_down[0].astype(BF16)], [q_rank], _ep_rms,
                   [(b_q_latent_norm[0].reshape(1, q_rank), (1, q_rank), lambda i, j: (0, 0))],
                   [(q_rank, BF16, q_rank)], tm=tm, name="q_down")
    w_q_cat = jnp.concatenate(
        [b_w_q_up[0], b_w_q_rope[0],
         jnp.zeros((q_rank, n_heads, Q_HEAD_STRIDE - QK_NOPE - QK_ROPE), F32)],
        axis=-1).reshape(q_rank, n_heads * Q_HEAD_STRIDE).astype(BF16)
    tn = min(1024, n_heads * Q_HEAD_STRIDE)
    q_cat, = _matmul(c_q, [w_q_cat], [tn], _ep_q_up, rope_specs,
                     [(n_heads * Q_HEAD_STRIDE, BF16, tn)], tm=tm, name="q_up")
    o = _attention(q_cat, k_cat, v, batch, seq, n_heads)
    x3, = mm_residual(o, b_w_out[0].astype(BF16), x2, "attn_out")

    xn_moe, top_idx, gates = _moe_router(x3, moe_norm[0], moe_w_router[0])
    dest, row_tok, block_e, block_valid, n_used = _moe_routing(top_idx, n_experts, MOE_BLOCK)
    x_rows = _moe_gather(xn_moe, row_tok, n_used, MOE_BLOCK)
    y_rows = _moe_experts(x_rows, block_e, block_valid, moe_gate16.reshape(n_experts, d, f_exp),
                          moe_up16.reshape(n_experts, d, f_exp), moe_w_down[0], MOE_BLOCK)
    out = _moe_combine(x3, y_rows, dest, gates, final_norm)
    return out.reshape(batch, seq, d)
```

```python
import functools

import jax
import jax.numpy as jnp
from jax import lax
from jax.experimental import pallas as pl
from jax.experimental.pallas import tpu as pltpu

F32 = jnp.float32
BF16 = jnp.bfloat16

CHUNK = 64
LRU_BLOCK = 256
CONV_WIDTH = 4
LRU_C = 8.0
QK_NOPE = 128
QK_ROPE = 64
V_DIM = 128
ROPE_THETA = 10000.0
ATTN_SCALE = (QK_NOPE + QK_ROPE) ** -0.5
TOP_K = 2
EPS = 1e-6

VMEM_LIMIT_BYTES = 56 * 1024 * 1024
LANES = 128
SUBLANES = 8
BF16_SUBLANES = 16
Q_HEAD_STRIDE = 2 * LANES
V_HEAD_STRIDE = 2 * V_DIM
NEG_BIG = -1e30
LOG2_E = 1.4426950408889634

MOE_BLOCK = 512


def _cparams(*sem):
    return pltpu.CompilerParams(dimension_semantics=sem, vmem_limit_bytes=VMEM_LIMIT_BYTES)


def _rms(x, g):
    return x * lax.rsqrt(jnp.mean(x * x, axis=-1, keepdims=True) + EPS) * g


def _norm_kernel(n_out, x_ref, *refs):
    g_refs, o_refs = refs[:n_out], refs[n_out:]
    x = x_ref[...]
    y = x * lax.rsqrt(jnp.mean(x * x, axis=-1, keepdims=True) + EPS)
    for g_ref, o_ref in zip(g_refs, o_refs):
        o_ref[...] = (y * g_ref[...]).astype(o_ref.dtype)


def _norm(x, gains, out_dtype=BF16, tm=256):
    m, d = x.shape
    tm = min(tm, m)
    n = len(gains)
    row = pl.BlockSpec((tm, d), lambda i: (i, 0))
    vec = pl.BlockSpec((1, d), lambda i: (0, 0))
    outs = pl.pallas_call(
        functools.partial(_norm_kernel, n),
        out_shape=[jax.ShapeDtypeStruct((m, d), out_dtype)] * n,
        grid=(m // tm,),
        in_specs=[row] + [vec] * n,
        out_specs=[row] * n,
        compiler_params=_cparams("parallel"),
        name="rmsnorm",
    )(x, *[g.reshape(1, d) for g in gains])
    return outs


def _mm_kernel(nw, ne, rider_gain, out_kinds, nk, epilogue, a_ref, *refs):
    nr, ng, no = len(rider_gain), sum(rider_gain), len(out_kinds)
    refs = list(refs)
    take = lambda n: [refs.pop(0) for _ in range(n)]
    w_refs, e_refs, rin_refs, gain_refs = take(nw), take(ne), take(nr), take(ng)
    o_refs, rout_refs, acc_refs = take(no), take(nr), refs

    gains = iter(gain_refs)
    for rin, rout, has_gain in zip(rin_refs, rout_refs, rider_gain):
        w = rin[...] * next(gains)[...] if has_gain else rin[...]
        rout[...] = w.astype(rout.dtype)

    def finish(accs):
        vals = epilogue(accs, e_refs)
        for o_ref, v, kind in zip(o_refs, vals, out_kinds):
            if kind == "tile":
                o_ref[...] = v.astype(o_ref.dtype)
                continue
            j = pl.program_id(1)
            v = jnp.broadcast_to(v, o_ref.shape)

            @pl.when(j == 0)
            def _():
                o_ref[...] = v

            @pl.when(j > 0)
            def _():
                o_ref[...] += v

    if nk == 1:
        finish([jnp.dot(a_ref[...], w[...], preferred_element_type=F32) for w in w_refs])
        return

    k = pl.program_id(2)

    @pl.when(k == 0)
    def _():
        for acc in acc_refs:
            acc[...] = jnp.zeros_like(acc)

    for acc, w in zip(acc_refs, w_refs):
        acc[...] += jnp.dot(a_ref[...], w[...], preferred_element_type=F32)

    @pl.when(k == nk - 1)
    def _():
        finish([acc[...] for acc in acc_refs])


def _rider_slab(rows, n_steps):
    slab = -(-rows // n_steps)
    slab = -(-slab // BF16_SUBLANES) * BF16_SUBLANES
    while rows % slab:
        slab += BF16_SUBLANES
    return slab


def _matmul(a, ws, tns, epilogue, extras, outs, *, tm, tk=None, riders=(), name):
    m, kdim = a.shape
    tm = min(tm, m)
    tk = kdim if tk is None else min(tk, kdim)
    nk = kdim // tk
    nj = ws[0].shape[1] // tns[0]
    n_steps = (m // tm) * nj * nk
    in_specs = [pl.BlockSpec((tm, tk), lambda i, j, k: (i, k))]
    for w, tn in zip(ws, tns):
        assert w.shape[1] // tn == nj
        in_specs.append(pl.BlockSpec((tk, tn), lambda i, j, k: (k, j)))
    for _, bs, im in extras:
        in_specs.append(pl.BlockSpec(bs, functools.partial(lambda im, i, j, k: im(i, j), im)))
    out_shape, out_specs, out_kinds = [], [], []
    for o in outs:
        if o[0] == "rowsum":
            out_shape.append(jax.ShapeDtypeStruct((m, LANES), F32))
            out_specs.append(pl.BlockSpec((tm, LANES), lambda i, j, k: (i, 0)))
            out_kinds.append("rowsum")
        else:
            n, dt, tn = o
            out_shape.append(jax.ShapeDtypeStruct((m, n), dt))
            out_specs.append(pl.BlockSpec((tm, tn), lambda i, j, k: (i, j)))
            out_kinds.append("tile")
    rider_specs, gain_specs, gain_args = [], [], []
    for r, gain in riders:
        rows, cols = r.shape
        slab = _rider_slab(rows, n_steps)
        slab_map = functools.partial(
            lambda last, i, j, k: (jnp.minimum((i * nj + j) * nk + k, last), 0), rows // slab - 1)
        rider_specs.append(pl.BlockSpec((slab, cols), slab_map))
        out_shape.append(jax.ShapeDtypeStruct((rows, cols), BF16))
        if gain is not None:
            gain_specs.append(pl.BlockSpec((slab, 1), slab_map))
            gain_args.append(gain.reshape(rows, 1))
    scratch = [pltpu.VMEM((tm, tn), F32) for tn in tns] if nk > 1 else []
    sequential = bool(riders) or "rowsum" in out_kinds
    semantics = ("arbitrary",) * 3 if sequential else ("parallel", "parallel", "arbitrary")
    res = pl.pallas_call(
        functools.partial(_mm_kernel, len(ws), len(extras), tuple(g is not None for _, g in riders),
                          tuple(out_kinds), nk, epilogue),
        out_shape=out_shape,
        grid=(m // tm, nj, nk),
        in_specs=in_specs + rider_specs + gain_specs,
        out_specs=out_specs + rider_specs,
        scratch_shapes=scratch,
        compiler_params=_cparams(*semantics),
        name=name,
    )(a, *ws, *[e[0] for e in extras], *[r for r, _ in riders], *gain_args)
    return res


def _ep_identity(accs, e_refs):
    return accs


def _ep_residual(accs, e_refs):
    return [e_refs[0][...] + accs[0]]


def _ep_residual_stats(accs, e_refs):
    x_new = e_refs[0][...] + accs[0]
    return [x_new, x_new, jnp.sum(x_new * x_new, axis=-1, keepdims=True)]


def _inv_rms(ssq_ref, d):
    return lax.rsqrt(ssq_ref[...][:, 0:1] * (1.0 / d) + EPS)


def _ep_swiglu_normed(d, accs, e_refs):
    r = _inv_rms(e_refs[0], d)
    g, u = accs[0] * r, accs[1] * r
    return [g * jax.nn.sigmoid(g) * u]


def _ep_rms_normed(d, accs, e_refs):
    ssq_ref, g_ref = e_refs
    return [_rms(accs[0] * _inv_rms(ssq_ref, d), g_ref[...])]


def _rope_slab(x, cos_t, sin_t):
    lane = lax.broadcasted_iota(jnp.int32, x.shape, 1)
    swapped = jnp.where(lane < QK_ROPE // 2,
                        pltpu.roll(x, LANES - QK_ROPE // 2, 1),
                        pltpu.roll(x, QK_ROPE // 2, 1))
    return x * cos_t + swapped * sin_t


def _ep_kv_down(d, accs, e_refs):
    ssq_ref, g_ref, cos_ref, sin_ref = e_refs
    r = _inv_rms(ssq_ref, d)
    return [_rms(accs[0] * r, g_ref[...]), _rope_slab(accs[1] * r, cos_ref[...], sin_ref[...])]


def _ep_q_up(accs, e_refs):
    cos_ref, sin_ref = e_refs
    nope, rope = accs
    cos_t, sin_t = cos_ref[...], sin_ref[...]
    cols = []
    for h in range(nope.shape[1] // QK_NOPE):
        cols.append(nope[:, h * QK_NOPE:(h + 1) * QK_NOPE])
        pair = rope[:, (h // 2) * LANES:(h // 2 + 1) * LANES]
        if h % 2:
            pair = pltpu.roll(pair, LANES - QK_ROPE, 1)
        cols.append(_rope_slab(pair, cos_t, sin_t))
    return [jnp.concatenate(cols, axis=1)]


def _ep_kv_up(accs, e_refs):
    acc_k, acc_v = accs
    k_rope = e_refs[0][...].astype(F32)
    ones = jnp.ones((acc_v.shape[0], V_HEAD_STRIDE - V_DIM), F32)
    k_cols, v_cols = [], []
    for h in range(acc_k.shape[1] // QK_NOPE):
        k_cols += [acc_k[:, h * QK_NOPE:(h + 1) * QK_NOPE], k_rope]
        v_cols += [acc_v[:, h * V_DIM:(h + 1) * V_DIM], ones]
    return [jnp.concatenate(k_cols, axis=1), jnp.concatenate(v_cols, axis=1)]


def _rglru_kernel(ts, tc, gate_ref, rec_ref, cw_ref, cb_ref, wr_ref, br_ref, wi_ref, bi_ref,
                  lam_ref, ride32_ref, o_ref, ride16_ref, tail_ref, h_ref, a_scr, x_scr):
    t = pl.program_id(2)
    ride16_ref[...] = ride32_ref[...].astype(ride16_ref.dtype)

    @pl.when(t == 0)
    def _():
        tail_ref[...] = jnp.zeros_like(tail_ref)
        h_ref[...] = jnp.zeros_like(h_ref)

    x = rec_ref[0].astype(F32)
    tail = tail_ref[...]
    cw = cw_ref[...]
    row8 = lax.broadcasted_iota(jnp.int32, (SUBLANES, tc), 0)
    u = x * cw[CONV_WIDTH - 1:CONV_WIDTH, :] + cb_ref[...]
    for k in range(1, CONV_WIDTH):
        xs = pltpu.roll(x, k, 0)
        top = jnp.where(row8 < k, pltpu.roll(tail, k, 0), xs[0:SUBLANES])
        xs = jnp.concatenate([top, xs[SUBLANES:]], axis=0)
        u = u + xs * cw[CONV_WIDTH - 1 - k:CONV_WIDTH - k, :]
    tail_ref[...] = x[ts - SUBLANES:ts]

    ub = u.astype(BF16)
    r_parts, i_parts = [], []
    for g in range(tc // LRU_BLOCK):
        ug = ub[:, g * LRU_BLOCK:(g + 1) * LRU_BLOCK]
        r_parts.append(jnp.dot(ug, wr_ref[g], preferred_element_type=F32))
        i_parts.append(jnp.dot(ug, wi_ref[g], preferred_element_type=F32))
    r = jax.nn.sigmoid(jnp.concatenate(r_parts, axis=1) + br_ref[...])
    ig = jax.nn.sigmoid(jnp.concatenate(i_parts, axis=1) + bi_ref[...])

    neg_lam = -lam_ref[...]
    softplus = jnp.maximum(neg_lam, 0.0) + jnp.log1p(jnp.exp(-jnp.abs(neg_lam)))
    log_a = (-LRU_C) * r * softplus
    a = jnp.exp(log_a)
    z = -jnp.tanh(log_a) * (1.0 + a * a)
    xin = (ig * u) * jnp.where(z > 0.0, z * lax.rsqrt(z), 0.0)

    nb = ts // SUBLANES
    grow = lax.broadcasted_iota(jnp.int32, (nb, LANES), 0)
    gate = jax.nn.gelu(gate_ref[0].astype(F32), approximate=True)
    for c in range(tc // LANES):
        lanes = slice(c * LANES, (c + 1) * LANES)
        a_scr[c] = a[:, lanes]
        x_scr[c] = xin[:, lanes]
        decay, state = [], []
        for r in range(SUBLANES):
            a_r = a_scr[c, pl.ds(r, nb, stride=SUBLANES), :]
            x_r = x_scr[c, pl.ds(r, nb, stride=SUBLANES), :]
            decay.append(a_r if r == 0 else a_r * decay[-1])
            state.append(x_r if r == 0 else a_r * state[-1] + x_r)
        ca, cx = decay[-1], state[-1]
        shift = 1
        while shift < nb:
            if shift < SUBLANES:
                keep = grow >= shift
                a_s = jnp.where(keep, pltpu.roll(ca, shift, 0), 1.0)
                x_s = jnp.where(keep, pltpu.roll(cx, shift, 0), 0.0)
            else:
                a_s = jnp.concatenate([jnp.ones((shift, LANES), F32), ca[:nb - shift]], axis=0)
                x_s = jnp.concatenate([jnp.zeros((shift, LANES), F32), cx[:nb - shift]], axis=0)
            cx = ca * x_s + cx
            ca = ca * a_s
            shift *= 2
        h_prev = h_ref[0:1, lanes]
        group_out = ca * h_prev + cx
        group_in = jnp.where(grow == 0, h_prev, pltpu.roll(group_out, 1, 0))
        for r in range(SUBLANES):
            x_scr[c, pl.ds(r, nb, stride=SUBLANES), :] = decay[r] * group_in + state[r]
        h_ref[:, lanes] = jnp.broadcast_to(group_out[nb - 1:nb, :], (SUBLANES, LANES))
        o_ref[0, :, lanes] = (gate[:, lanes] * x_scr[c]).astype(o_ref.dtype)


def _rglru(gr, conv_w, conv_b, w_r, b_r, w_i, b_i, lam, batch, seq, rider32, *, ts=512, tc=512):
    d = conv_w.shape[1]
    ts, tc = min(ts, seq), min(tc, d)
    nc, nt = d // tc, seq // ts
    gpb = tc // LRU_BLOCK
    gr3 = gr.reshape(batch, seq, 2 * d)
    vec = pl.BlockSpec((1, tc), lambda b, c, t: (0, c))
    wspec = pl.BlockSpec((gpb, LRU_BLOCK, LRU_BLOCK), lambda b, c, t: (c, 0, 0))
    r_rows, r_cols = rider32.shape
    slab = _rider_slab(r_rows, batch * nc * nt)
    rider = pl.BlockSpec((slab, r_cols), functools.partial(
        lambda last, b, c, t: (jnp.minimum((b * nc + c) * nt + t, last), 0), r_rows // slab - 1))
    out, rider16 = pl.pallas_call(
        functools.partial(_rglru_kernel, ts, tc),
        out_shape=[jax.ShapeDtypeStruct((batch, seq, d), BF16),
                   jax.ShapeDtypeStruct((r_rows, r_cols), BF16)],
        grid=(batch, nc, nt),
        in_specs=[
            pl.BlockSpec((1, ts, tc), lambda b, c, t: (b, t, c)),
            pl.BlockSpec((1, ts, tc), lambda b, c, t: (b, t, nc + c)),
            pl.BlockSpec((CONV_WIDTH, tc), lambda b, c, t: (0, c)),
            vec, wspec, vec, wspec, vec, vec, rider,
        ],
        out_specs=[pl.BlockSpec((1, ts, tc), lambda b, c, t: (b, t, c)), rider],
        scratch_shapes=[pltpu.VMEM((SUBLANES, tc), F32), pltpu.VMEM((SUBLANES, tc), F32),
                        pltpu.VMEM((tc // LANES, ts, LANES), F32),
                        pltpu.VMEM((tc // LANES, ts, LANES), F32)],
        compiler_params=_cparams("arbitrary", "arbitrary", "arbitrary"),
        name="rglru",
    )(gr3, gr3, conv_w, conv_b.reshape(1, d), w_r.astype(BF16), b_r.reshape(1, d),
      w_i.astype(BF16), b_i.reshape(1, d), lam.reshape(1, d), rider32)
    return out.reshape(batch * seq, d), rider16


def _attn_kernel(tq, hps, q_ref, k_ref, v_ref, o_ref, sa_ref, sb_ref, m_ref, acc_ref):
    qi = pl.program_id(2)
    m_ref[...] = jnp.full_like(m_ref, NEG_BIG)
    acc_ref[...] = jnp.zeros_like(acc_ref)
    contract_last = (((1,), (1,)), ((), ()))

    def scores(j, s_ref):
        k0 = pl.multiple_of(j * tq, tq)
        for h in range(hps):
            q = q_ref[0, :, h * Q_HEAD_STRIDE:(h + 1) * Q_HEAD_STRIDE]
            k = k_ref[0, pl.ds(k0, tq), h * Q_HEAD_STRIDE:(h + 1) * Q_HEAD_STRIDE]
            s_ref[h] = lax.dot_general(q, k, contract_last, preferred_element_type=F32)

    def consume(j, s_ref, masked):
        k0 = pl.multiple_of(j * tq, tq)
        for h in range(hps):
            s = s_ref[h]
            if masked:
                qc = lax.broadcasted_iota(jnp.int32, s.shape, 0) // CHUNK
                kc = lax.broadcasted_iota(jnp.int32, s.shape, 1) // CHUNK
                s = jnp.where(qc >= kc, s, NEG_BIG)
            m_prev = m_ref[h]
            m_next = jnp.maximum(m_prev, jnp.max(s, axis=-1, keepdims=True))
            p = jnp.exp2(s - jnp.tile(m_next, (1, tq // LANES)))
            alpha = jnp.exp2(m_prev - m_next)
            m_ref[h] = m_next
            v = v_ref[0, pl.ds(k0, tq), h * V_HEAD_STRIDE:(h + 1) * V_HEAD_STRIDE]
            acc_ref[h] = (jnp.tile(alpha, (1, V_HEAD_STRIDE // LANES)) * acc_ref[h]
                          + jnp.dot(p.astype(BF16), v, preferred_element_type=F32))

    scores(0, sa_ref)

    def pair(t, carry):
        scores(2 * t + 1, sb_ref)
        consume(2 * t, sa_ref, False)
        scores(2 * t + 2, sa_ref)
        consume(2 * t + 1, sb_ref, False)
        return carry

    lax.fori_loop(0, qi // 2, pair, 0)

    @pl.when(qi % 2 == 1)
    def _():
        scores(qi, sb_ref)
        consume(qi - 1, sa_ref, False)
        consume(qi, sb_ref, True)

    @pl.when(qi % 2 == 0)
    def _():
        consume(qi, sa_ref, True)

    for h in range(hps):
        acc = acc_ref[h]
        o_ref[0, :, h * V_DIM:(h + 1) * V_DIM] = (acc[:, :V_DIM] / acc[:, V_DIM:]).astype(o_ref.dtype)


def _attention(q_cat, k_cat, v_ones, batch, seq, n_heads, *, tq=512, hps=4):
    tq = min(tq, seq)
    q3 = q_cat.reshape(batch, seq, n_heads * Q_HEAD_STRIDE)
    k3 = k_cat.reshape(batch, seq, n_heads * Q_HEAD_STRIDE)
    v3 = v_ones.reshape(batch, seq, n_heads * V_HEAD_STRIDE)
    out = pl.pallas_call(
        functools.partial(_attn_kernel, tq, hps),
        out_shape=jax.ShapeDtypeStruct((batch, seq, n_heads * V_DIM), BF16),
        grid=(batch, n_heads // hps, seq // tq),
        in_specs=[
            pl.BlockSpec((1, tq, hps * Q_HEAD_STRIDE), lambda b, h, i: (b, i, h)),
            pl.BlockSpec((1, seq, hps * Q_HEAD_STRIDE), lambda b, h, i: (b, 0, h)),
            pl.BlockSpec((1, seq, hps * V_HEAD_STRIDE), lambda b, h, i: (b, 0, h)),
        ],
        out_specs=pl.BlockSpec((1, tq, hps * V_DIM), lambda b, h, i: (b, i, h)),
        scratch_shapes=[pltpu.VMEM((hps, tq, tq), F32), pltpu.VMEM((hps, tq, tq), F32),
                        pltpu.VMEM((hps, tq, LANES), F32),
                        pltpu.VMEM((hps, tq, V_HEAD_STRIDE), F32)],
        compiler_params=_cparams("parallel", "parallel", "arbitrary"),
        name="attention",
    )(q3, k3, v3)
    return out.reshape(batch * seq, n_heads * V_DIM)


def _moe_router_kernel(x_ref, g_ref, wh_ref, wl_ref, xn_ref, idx_ref, gate_ref):
    x = x_ref[...]
    xn = _rms(x, g_ref[...])
    xn_ref[...] = xn
    xh = xn.astype(BF16)
    xl = (xn - xh.astype(F32)).astype(BF16)
    wh, wl = wh_ref[...], wl_ref[...]
    logits = (jnp.dot(xh, wh, preferred_element_type=F32) + jnp.dot(xh, wl, preferred_element_type=F32)
              + jnp.dot(xl, wh, preferred_element_type=F32))
    n_e = logits.shape[1]
    lane = lax.broadcasted_iota(jnp.int32, logits.shape, 1)
    m1 = jnp.max(logits, axis=-1, keepdims=True)
    i1 = jnp.min(jnp.where(logits == m1, lane, n_e), axis=-1, keepdims=True)
    rest = jnp.where(lane == i1, -jnp.inf, logits)
    m2 = jnp.max(rest, axis=-1, keepdims=True)
    i2 = jnp.min(jnp.where(rest == m2, lane, n_e), axis=-1, keepdims=True)
    e2 = jnp.exp(m2 - m1)
    denom = 1.0 + e2
    slot = lax.broadcasted_iota(jnp.int32, idx_ref.shape, 1)
    idx_ref[...] = jnp.where(slot == 0, i1, i2)
    gate_ref[...] = jnp.where(slot == 0, 1.0 / denom, e2 / denom)


def _moe_router(x, g, w_router, tm=256):
    m, d = x.shape
    tm = min(tm, m)
    n_e = w_router.shape[1]
    w_hi = w_router.astype(BF16)
    row = pl.BlockSpec((tm, d), lambda i: (i, 0))
    small = pl.BlockSpec((tm, TOP_K), lambda i: (i, 0))
    return pl.pallas_call(
        _moe_router_kernel,
        out_shape=[jax.ShapeDtypeStruct((m, d), F32),
                   jax.ShapeDtypeStruct((m, TOP_K), jnp.int32),
                   jax.ShapeDtypeStruct((m, TOP_K), F32)],
        grid=(m // tm,),
        in_specs=[row, pl.BlockSpec((1, d), lambda i: (0, 0)),
                  pl.BlockSpec((d, n_e), lambda i: (0, 0)),
                  pl.BlockSpec((d, n_e), lambda i: (0, 0))],
        out_specs=[row, small, small],
        compiler_params=_cparams("parallel"),
        name="moe_norm_router",
    )(x, g.reshape(1, d), w_hi, (w_router - w_hi.astype(F32)).astype(BF16))


def _row_copy(src_hbm, src_row, buf, dst_row, sem):
    return pltpu.make_async_copy(src_hbm.at[pl.ds(src_row, 1)], buf.at[pl.ds(dst_row, 1)], sem)


DMA_LOOP_UNROLL = 8


def _moe_gather_kernel(blk, tok_ref, nused_ref, x_hbm, o_ref, buf, sem):
    b = pl.program_id(0)
    n_used = nused_ref[0]

    def fetch(block, slot):
        def issue(r, c):
            _row_copy(x_hbm, tok_ref[block * blk + r], buf.at[slot], r, sem.at[slot]).start()
            return c

        lax.fori_loop(0, blk, issue, 0, unroll=DMA_LOOP_UNROLL)

    @pl.when((b == 0) & (n_used > 0))
    def _():
        fetch(0, 0)

    @pl.when(b + 1 < n_used)
    def _():
        fetch(b + 1, (b + 1) % 2)

    @pl.when(b < n_used)
    def _():
        slot = b % 2

        def drain(r, c):
            _row_copy(x_hbm, 0, buf.at[slot], r, sem.at[slot]).wait()
            return c

        lax.fori_loop(0, blk, drain, 0, unroll=DMA_LOOP_UNROLL)
        o_ref[...] = buf[slot].astype(o_ref.dtype)

    @pl.when(b >= n_used)
    def _():
        o_ref[...] = jnp.zeros_like(o_ref)


def _moe_gather(xn, row_tok, n_used, blk):
    n_rows = row_tok.shape[0]
    d = xn.shape[1]
    return pl.pallas_call(
        functools.partial(_moe_gather_kernel, blk),
        out_shape=jax.ShapeDtypeStruct((n_rows, d), BF16),
        grid_spec=pltpu.PrefetchScalarGridSpec(
            num_scalar_prefetch=2,
            grid=(n_rows // blk,),
            in_specs=[pl.BlockSpec(memory_space=pl.ANY)],
            out_specs=pl.BlockSpec((blk, d), lambda b, tok, nu: (b, 0)),
            scratch_shapes=[pltpu.VMEM((2, blk, d), F32), pltpu.SemaphoreType.DMA((2,))],
        ),
        compiler_params=_cparams("arbitrary"),
        name="moe_gather",
    )(row_tok, n_used, xn)


def _for_filled_rows(blk, n_valid, compute, o_ref):
    half = blk // 2

    @pl.when(n_valid > half)
    def _():
        o_ref[...] = compute(slice(0, blk)).astype(o_ref.dtype)

    @pl.when((n_valid > 0) & (n_valid <= half))
    def _():
        o_ref[0:half, :] = compute(slice(0, half)).astype(o_ref.dtype)
        o_ref[half:blk, :] = jnp.zeros((blk - half, o_ref.shape[1]), o_ref.dtype)

    @pl.when(n_valid == 0)
    def _():
        o_ref[...] = jnp.zeros_like(o_ref)


def _moe_up_kernel(blk, be_ref, valid_ref, x_ref, wg_ref, wu_ref, wd32_ref, o_ref, wd16_ref):
    wd16_ref[...] = wd32_ref[...].astype(wd16_ref.dtype)

    def compute(rows):
        x = x_ref[rows, :]
        g = jnp.dot(x, wg_ref[0], preferred_element_type=F32)
        u = jnp.dot(x, wu_ref[0], preferred_element_type=F32)
        return g * jax.nn.sigmoid(g) * u

    _for_filled_rows(blk, valid_ref[pl.program_id(0)], compute, o_ref)


def _moe_down_kernel(blk, be_ref, valid_ref, h_ref, wd_ref, o_ref):
    def compute(rows):
        return jnp.dot(h_ref[rows, :], wd_ref[0], preferred_element_type=F32)

    _for_filled_rows(blk, valid_ref[pl.program_id(0)], compute, o_ref)


def _moe_experts(x_rows, block_e, block_valid, w_gate, w_up, w_down_f32, blk, *, tf=1024, tn=1024):
    n_rows, d = x_rows.shape
    n_e, _, f = w_gate.shape
    tf, tn = min(tf, f), min(tn, d)
    nf, nn = f // tf, d // tn
    n_blocks = n_rows // blk

    def wcol(nj):
        return lambda b, j, be, nv: (be[b], 0, jnp.where(nv[b] > 0, j, nj - 1))

    wd2 = w_down_f32.reshape(n_e * f, d)
    slab = _rider_slab(n_e * f, n_blocks * nf)
    rider = pl.BlockSpec((slab, d), functools.partial(
        lambda last, b, j, be, nu: (jnp.minimum(b * nf + j, last), 0), n_e * f // slab - 1))
    h_rows, w_down = pl.pallas_call(
        functools.partial(_moe_up_kernel, blk),
        out_shape=[jax.ShapeDtypeStruct((n_rows, f), BF16), jax.ShapeDtypeStruct((n_e * f, d), BF16)],
        grid_spec=pltpu.PrefetchScalarGridSpec(
            num_scalar_prefetch=2,
            grid=(n_blocks, nf),
            in_specs=[pl.BlockSpec((blk, d), lambda b, j, be, nu: (b, 0)),
                      pl.BlockSpec((1, d, tf), wcol(nf)),
                      pl.BlockSpec((1, d, tf), wcol(nf)),
                      rider],
            out_specs=[pl.BlockSpec((blk, tf), lambda b, j, be, nu: (b, j)), rider],
        ),
        compiler_params=_cparams("arbitrary", "arbitrary"),
        name="moe_gate_up",
    )(block_e, block_valid, x_rows, w_gate, w_up, wd2)
    w_down = w_down.reshape(n_e, f, d)
    y_rows = pl.pallas_call(
        functools.partial(_moe_down_kernel, blk),
        out_shape=jax.ShapeDtypeStruct((n_rows, d), F32),
        grid_spec=pltpu.PrefetchScalarGridSpec(
            num_scalar_prefetch=2,
            grid=(n_rows // blk, nn),
            in_specs=[pl.BlockSpec((blk, f), lambda b, j, be, nu: (b, 0)),
                      pl.BlockSpec((1, f, tn), wcol(nn))],
            out_specs=pl.BlockSpec((blk, tn), lambda b, j, be, nu: (b, j)),
        ),
        compiler_params=_cparams("arbitrary", "arbitrary"),
        name="moe_down",
    )(block_e, block_valid, h_rows, w_down)
    return y_rows


def _moe_combine_kernel(tt, pos_ref, x_ref, gate_ref, g_ref, y_hbm, o_ref, buf, sem):
    i = pl.program_id(0)
    n_tiles = pl.num_programs(0)

    def fetch(tile, slot):
        def issue(r, c):
            for s in range(TOP_K):
                _row_copy(y_hbm, pos_ref[(tile * tt + r) * TOP_K + s], buf.at[slot, s], r,
                          sem.at[slot]).start()
            return c

        lax.fori_loop(0, tt, issue, 0, unroll=DMA_LOOP_UNROLL)

    @pl.when(i == 0)
    def _():
        fetch(0, 0)

    @pl.when(i + 1 < n_tiles)
    def _():
        fetch(i + 1, (i + 1) % 2)

    slot = i % 2

    def drain(r, c):
        for s in range(TOP_K):
            _row_copy(y_hbm, 0, buf.at[slot, s], r, sem.at[slot]).wait()
        return c

    lax.fori_loop(0, tt, drain, 0, unroll=DMA_LOOP_UNROLL)
    gates = gate_ref[...]
    y = buf[slot, 0] * gates[:, 0:1] + buf[slot, 1] * gates[:, 1:2]
    o_ref[...] = _rms(x_ref[...] + y, g_ref[...])


def _moe_combine(x, y_rows, pos, gates, g, tt=256):
    m, d = x.shape
    tt = min(tt, m)
    row = lambda i, p: (i, 0)
    return pl.pallas_call(
        functools.partial(_moe_combine_kernel, tt),
        out_shape=jax.ShapeDtypeStruct((m, d), F32),
        grid_spec=pltpu.PrefetchScalarGridSpec(
            num_scalar_prefetch=1,
            grid=(m // tt,),
            in_specs=[pl.BlockSpec((tt, d), row),
                      pl.BlockSpec((tt, TOP_K), row),
                      pl.BlockSpec((1, d), lambda i, p: (0, 0)),
                      pl.BlockSpec(memory_space=pl.ANY)],
            out_specs=pl.BlockSpec((tt, d), row),
            scratch_shapes=[pltpu.VMEM((2, TOP_K, tt, d), F32), pltpu.SemaphoreType.DMA((2,))],
        ),
        compiler_params=_cparams("arbitrary"),
        name="moe_combine_norm",
    )(pos, x, gates, g.reshape(1, d), y_rows)


def _moe_routing(idx, n_experts, blk):
    n = idx.shape[0]
    flat_e = idx.reshape(-1)
    onehot = (flat_e[:, None] == jnp.arange(n_experts, dtype=jnp.int32)[None, :]).astype(jnp.int32)
    csum = jnp.cumsum(onehot, axis=0)
    rank = jnp.sum(csum * onehot, axis=1) - 1
    counts = csum[-1]
    padded = (counts + blk - 1) // blk * blk
    pend = jnp.cumsum(padded)
    pstart = pend - padded
    dest = (pstart[flat_e] + rank).astype(jnp.int32)
    n_rows = -(-(n * TOP_K + n_experts * (blk - 1)) // blk) * blk
    n_blocks = n_rows // blk
    flat_tok = jnp.arange(n * TOP_K, dtype=jnp.int32) // TOP_K
    row_tok = jnp.zeros((n_rows,), jnp.int32).at[dest].set(flat_tok)
    block_e = jnp.minimum(
        jnp.searchsorted(pend, jnp.arange(n_blocks, dtype=jnp.int32) * blk, side='right'),
        n_experts - 1).astype(jnp.int32)
    n_used = (pend[-1] // blk).astype(jnp.int32).reshape(1)
    block_start = jnp.arange(n_blocks, dtype=jnp.int32) * blk
    block_valid = jnp.clip(pstart[block_e] + counts[block_e] - block_start, 0, blk)
    block_valid = jnp.where(block_start < pend[-1], block_valid, 0).astype(jnp.int32)
    return dest, row_tok, block_e, block_valid, n_used


def _rope_tables(seq):
    inv_freq = ROPE_THETA ** (-jnp.arange(0, QK_ROPE, 2, dtype=F32) / QK_ROPE)
    ang = jnp.arange(seq, dtype=F32)[:, None] * inv_freq[None, :]
    cos, sin = jnp.cos(ang), jnp.sin(ang)
    pad = jnp.zeros((seq, LANES - QK_ROPE), F32)
    return (jnp.concatenate([cos, cos, pad], axis=1),
            jnp.concatenate([-sin, sin, pad], axis=1))


def kernel(x, a_norm, a_w_in, a_conv_w, a_conv_b, a_w_rgate, a_b_rgate, a_w_igate, a_b_igate, a_lambda, a_w_out, kv_norm, kv_w_down, kv_latent_norm, kv_w_rope, kv_w_up_k, kv_w_up_v, b_norm, b_w_q_down, b_q_latent_norm, b_w_q_up, b_w_q_rope, b_w_out, ffn_norm, ffn_w_gate, ffn_w_up, ffn_w_down, moe_norm, moe_w_router, moe_w_gate, moe_w_up, moe_w_down, final_norm):
    batch, seq, d = x.shape
    n = batch * seq
    n_heads = kv_w_up_k.shape[1]
    assert a_norm.shape[0] == 1 and b_norm.shape[0] == 1, "depth-2 block: one RG-LRU and one MLA layer"
    xf = x.reshape(n, d)
    tm = min(1024, seq)
    n_seq_tiles = seq // tm

    def mm_residual(a, w, res, name, riders=(), stats=False, tn=512, tk=None):
        tn = min(tn, w.shape[1])
        outs = [(w.shape[1], F32, tn)] + ([(w.shape[1], BF16, tn), ("rowsum",)] if stats else [])
        return _matmul(a, [w], [tn], _ep_residual_stats if stats else _ep_residual,
                       [(res, (tm, tn), lambda i, j: (i, j))], outs,
                       tm=tm, tk=tk, riders=riders, name=name)

    def row_stat(ssq):
        return (ssq, (tm, LANES), lambda i, j: (i, 0))

    n_experts, _, f_exp = moe_w_gate.shape[1:]
    moe_gate32 = moe_w_gate[0].reshape(n_experts * d, f_exp)
    moe_up32 = moe_w_up[0].reshape(n_experts * d, f_exp)

    xn, = _norm(xf, [a_norm[0]])
    w_in = a_w_in[0].astype(BF16)
    tn = min(1024, w_in.shape[1])
    gr, w_out_a, wg, wu = _matmul(
        xn, [w_in], [tn], _ep_identity, [], [(w_in.shape[1], BF16, tn)], tm=tm,
        riders=[(a_w_out[0], None), (ffn_w_gate[0], ffn_norm[0]), (ffn_w_up[0], ffn_norm[0])],
        name="rglru_in")
    gated, moe_up16 = _rglru(gr, a_conv_w[0], a_conv_b[0], a_w_rgate[0], a_b_rgate[0], a_w_igate[0],
                             a_b_igate[0], a_lambda[0], batch, seq, moe_up32)
    x1, x1_16, ssq1, wd = mm_residual(gated, w_out_a, xf, "rglru_out", riders=[(ffn_w_down[0], None)],
                                      stats=True)

    f = wg.shape[1]
    tf = min(512, f)
    hid, moe_gate16 = _matmul(x1_16, [wg, wu], [tf, tf], functools.partial(_ep_swiglu_normed, d),
                              [row_stat(ssq1)], [(f, BF16, tf)], tm=tm,
                              riders=[(moe_gate32, None)], name="ffn_gate_up")
    x2, x2_16, ssq2 = mm_residual(hid, wd, x1, "ffn_down", stats=True, tn=1024, tk=min(2048, f))

    cos_t, sin_t = _rope_tables(seq)
    rope_specs = [(cos_t, (tm, LANES), lambda i, j: (i % n_seq_tiles, 0)),
                  (sin_t, (tm, LANES), lambda i, j: (i % n_seq_tiles, 0))]
    kv_rank = kv_w_down.shape[1]
    w_kv_down = (kv_w_down * kv_norm[:, None]).astype(BF16)
    w_kv_rope = jnp.pad(kv_w_rope * kv_norm[:, None], ((0, 0), (0, LANES - QK_ROPE))).astype(BF16)
    c_kv, k_rope = _matmul(
        x2_16, [w_kv_down, w_kv_rope], [kv_rank, LANES], functools.partial(_ep_kv_down, d),
        [row_stat(ssq2), (kv_latent_norm.reshape(1, kv_rank), (1, kv_rank), lambda i, j: (0, 0))]
        + rope_specs,
        [(kv_rank, BF16, kv_rank), (LANES, BF16, LANES)], tm=tm, name="kv_down")
    hk = n_heads * QK_NOPE
    tn = min(1024, hk)
    k_cat, v = _matmul(
        c_kv, [kv_w_up_k.reshape(kv_rank, hk).astype(BF16),
               kv_w_up_v.reshape(kv_rank, n_heads * V_DIM).astype(BF16)],
        [tn, tn], _ep_kv_up, [(k_rope, (tm, LANES), lambda i, j: (i, 0))],
        [(n_heads * Q_HEAD_STRIDE, BF16, tn // QK_NOPE * Q_HEAD_STRIDE),
         (n_heads * V_HEAD_STRIDE, BF16, tn // V_DIM * V_HEAD_STRIDE)],
        tm=tm, name="kv_up")

    q_rank = b_w_q_down.shape[2]
    c_q, = _matmul(x2_16, [(b_w_q_down[0] * b_norm[0][:, None]).astype(BF16)], [q_rank],
                   functools.partial(_ep_rms_normed, d),
                   [row_stat(ssq2),
                    (b_q_latent_norm[0].reshape(1, q_rank), (1, q_rank), lambda i, j: (0, 0))],
                   [(q_rank, BF16, q_rank)], tm=tm, name="q_down")
    hq = min(8, n_heads)
    q_scale = ATTN_SCALE * LOG2_E
    q_rope_specs = [(t * q_scale, bs, im) for t, bs, im in rope_specs]
    q_cat, = _matmul(c_q, [(b_w_q_up[0] * q_scale).reshape(q_rank, n_heads * QK_NOPE).astype(BF16),
                           b_w_q_rope[0].reshape(q_rank, n_heads * QK_ROPE).astype(BF16)],
                     [hq * QK_NOPE, hq * QK_ROPE], _ep_q_up, q_rope_specs,
                     [(n_heads * Q_HEAD_STRIDE, BF16, hq * Q_HEAD_STRIDE)], tm=tm, name="q_up")
    o = _attention(q_cat, k_cat, v, batch, seq, n_heads)
    x3, = mm_residual(o, b_w_out[0].astype(BF16), x2, "attn_out")

    xn_moe, top_idx, gates = _moe_router(x3, moe_norm[0], moe_w_router[0])
    dest, row_tok, block_e, block_valid, n_used = _moe_routing(top_idx, n_experts, MOE_BLOCK)
    x_rows = _moe_gather(xn_moe, row_tok, n_used, MOE_BLOCK)
    y_rows = _moe_experts(x_rows, block_e, block_valid, moe_gate16.reshape(n_experts, d, f_exp),
                          moe_up16.reshape(n_experts, d, f_exp), moe_w_down[0], MOE_BLOCK)
    out = _moe_combine(x3, y_rows, dest, gates, final_norm)
    return out.reshape(batch, seq, d)
```

```python
import functools

import jax
import jax.numpy as jnp
from jax import lax
from jax.experimental import pallas as pl
from jax.experimental.pallas import tpu as pltpu

F32 = jnp.float32
BF16 = jnp.bfloat16

CHUNK = 64
LRU_BLOCK = 256
CONV_WIDTH = 4
LRU_C = 8.0
QK_NOPE = 128
QK_ROPE = 64
V_DIM = 128
ROPE_THETA = 10000.0
ATTN_SCALE = (QK_NOPE + QK_ROPE) ** -0.5
TOP_K = 2
EPS = 1e-6

VMEM_LIMIT_BYTES = 56 * 1024 * 1024
LANES = 128
SUBLANES = 8
BF16_SUBLANES = 16
Q_HEAD_STRIDE = 2 * LANES
V_HEAD_STRIDE = 2 * V_DIM
NEG_BIG = -1e30
LOG2_E = 1.4426950408889634

MOE_BLOCK = 512


def _cparams(*sem):
    return pltpu.CompilerParams(dimension_semantics=sem, vmem_limit_bytes=VMEM_LIMIT_BYTES)


def _rms(x, g):
    return x * lax.rsqrt(jnp.mean(x * x, axis=-1, keepdims=True) + EPS) * g


def _norm_kernel(n_out, x_ref, *refs):
    g_refs, o_refs = refs[:n_out], refs[n_out:]
    x = x_ref[...]
    y = x * lax.rsqrt(jnp.mean(x * x, axis=-1, keepdims=True) + EPS)
    for g_ref, o_ref in zip(g_refs, o_refs):
        o_ref[...] = (y * g_ref[...]).astype(o_ref.dtype)


def _norm(x, gains, out_dtype=BF16, tm=256):
    m, d = x.shape
    tm = min(tm, m)
    n = len(gains)
    row = pl.BlockSpec((tm, d), lambda i: (i, 0))
    vec = pl.BlockSpec((1, d), lambda i: (0, 0))
    outs = pl.pallas_call(
        functools.partial(_norm_kernel, n),
        out_shape=[jax.ShapeDtypeStruct((m, d), out_dtype)] * n,
        grid=(m // tm,),
        in_specs=[row] + [vec] * n,
        out_specs=[row] * n,
        compiler_params=_cparams("parallel"),
        name="rmsnorm",
    )(x, *[g.reshape(1, d) for g in gains])
    return outs


def _mm_kernel(nw, ne, rider_gain, out_kinds, nk, epilogue, a_ref, *refs):
    nr, ng, no = len(rider_gain), sum(rider_gain), len(out_kinds)
    refs = list(refs)
    take = lambda n: [refs.pop(0) for _ in range(n)]
    w_refs, e_refs, rin_refs, gain_refs = take(nw), take(ne), take(nr), take(ng)
    o_refs, rout_refs, acc_refs = take(no), take(nr), refs

    gains = iter(gain_refs)
    for rin, rout, has_gain in zip(rin_refs, rout_refs, rider_gain):
        w = rin[...] * next(gains)[...] if has_gain else rin[...]
        rout[...] = w.astype(rout.dtype)

    def finish(accs):
        vals = epilogue(accs, e_refs)
        for o_ref, v, kind in zip(o_refs, vals, out_kinds):
            if kind == "tile":
                o_ref[...] = v.astype(o_ref.dtype)
                continue
            j = pl.program_id(1)
            v = jnp.broadcast_to(v, o_ref.shape)

            @pl.when(j == 0)
            def _():
                o_ref[...] = v

            @pl.when(j > 0)
            def _():
                o_ref[...] += v

    if nk == 1:
        finish([jnp.dot(a_ref[...], w[...], preferred_element_type=F32) for w in w_refs])
        return

    k = pl.program_id(2)

    @pl.when(k == 0)
    def _():
        for acc in acc_refs:
            acc[...] = jnp.zeros_like(acc)

    for acc, w in zip(acc_refs, w_refs):
        acc[...] += jnp.dot(a_ref[...], w[...], preferred_element_type=F32)

    @pl.when(k == nk - 1)
    def _():
        finish([acc[...] for acc in acc_refs])


def _rider_slab(rows, n_steps):
    slab = -(-rows // n_steps)
    slab = -(-slab // BF16_SUBLANES) * BF16_SUBLANES
    while rows % slab:
        slab += BF16_SUBLANES
    return slab


def _matmul(a, ws, tns, epilogue, extras, outs, *, tm, tk=None, riders=(), name):
    m, kdim = a.shape
    tm = min(tm, m)
    tk = kdim if tk is None else min(tk, kdim)
    assert m % tm == 0 and kdim % tk == 0, (m, tm, kdim, tk)
    nk = kdim // tk
    nj = ws[0].shape[1] // tns[0]
    n_steps = (m // tm) * nj * nk
    in_specs = [pl.BlockSpec((tm, tk), lambda i, j, k: (i, k))]
    for w, tn in zip(ws, tns):
        assert w.shape[1] // tn == nj
        in_specs.append(pl.BlockSpec((tk, tn), lambda i, j, k: (k, j)))
    for _, bs, im in extras:
        in_specs.append(pl.BlockSpec(bs, functools.partial(lambda im, i, j, k: im(i, j), im)))
    out_shape, out_specs, out_kinds = [], [], []
    for o in outs:
        if o[0] == "rowsum":
            out_shape.append(jax.ShapeDtypeStruct((m, LANES), F32))
            out_specs.append(pl.BlockSpec((tm, LANES), lambda i, j, k: (i, 0)))
            out_kinds.append("rowsum")
        else:
            n, dt, tn = o
            out_shape.append(jax.ShapeDtypeStruct((m, n), dt))
            out_specs.append(pl.BlockSpec((tm, tn), lambda i, j, k: (i, j)))
            out_kinds.append("tile")
    rider_specs, gain_specs, gain_args = [], [], []
    for r, gain in riders:
        rows, cols = r.shape
        slab = _rider_slab(rows, n_steps)
        slab_map = functools.partial(
            lambda last, i, j, k: (jnp.minimum((i * nj + j) * nk + k, last), 0), rows // slab - 1)
        rider_specs.append(pl.BlockSpec((slab, cols), slab_map))
        out_shape.append(jax.ShapeDtypeStruct((rows, cols), BF16))
        if gain is not None:
            gain_specs.append(pl.BlockSpec((slab, 1), slab_map))
            gain_args.append(gain.reshape(rows, 1))
    scratch = [pltpu.VMEM((tm, tn), F32) for tn in tns] if nk > 1 else []
    sequential = bool(riders) or "rowsum" in out_kinds
    semantics = ("arbitrary",) * 3 if sequential else ("parallel", "parallel", "arbitrary")
    res = pl.pallas_call(
        functools.partial(_mm_kernel, len(ws), len(extras), tuple(g is not None for _, g in riders),
                          tuple(out_kinds), nk, epilogue),
        out_shape=out_shape,
        grid=(m // tm, nj, nk),
        in_specs=in_specs + rider_specs + gain_specs,
        out_specs=out_specs + rider_specs,
        scratch_shapes=scratch,
        compiler_params=_cparams(*semantics),
        name=name,
    )(a, *ws, *[e[0] for e in extras], *[r for r, _ in riders], *gain_args)
    return res


def _ep_identity(accs, e_refs):
    return accs


def _ep_residual(accs, e_refs):
    return [e_refs[0][...] + accs[0]]


def _ep_residual_stats(accs, e_refs):
    x_new = e_refs[0][...] + accs[0]
    return [x_new, x_new, jnp.sum(x_new * x_new, axis=-1, keepdims=True)]


def _inv_rms(ssq_ref, d):
    return lax.rsqrt(ssq_ref[...][:, 0:1] * (1.0 / d) + EPS)


def _ep_swiglu_normed(d, accs, e_refs):
    r = _inv_rms(e_refs[0], d)
    g, u = accs[0] * r, accs[1] * r
    return [g * jax.nn.sigmoid(g) * u]


def _ep_rms_normed(d, accs, e_refs):
    ssq_ref, g_ref = e_refs
    return [_rms(accs[0] * _inv_rms(ssq_ref, d), g_ref[...])]


def _rope_slab(x, cos_t, sin_t):
    lane = lax.broadcasted_iota(jnp.int32, x.shape, 1)
    swapped = jnp.where(lane < QK_ROPE // 2,
                        pltpu.roll(x, LANES - QK_ROPE // 2, 1),
                        pltpu.roll(x, QK_ROPE // 2, 1))
    return x * cos_t + swapped * sin_t


def _ep_kv_down(d, accs, e_refs):
    ssq_ref, g_ref, cos_ref, sin_ref = e_refs
    r = _inv_rms(ssq_ref, d)
    return [_rms(accs[0] * r, g_ref[...]), _rope_slab(accs[1] * r, cos_ref[...], sin_ref[...])]


def _ep_q_up(accs, e_refs):
    cos_lo, sin_lo, cos_hi, sin_hi = (r[...] for r in e_refs)
    nope, rope, rope_sw = accs
    cols = []
    for h in range(nope.shape[1] // QK_NOPE):
        cols.append(nope[:, h * QK_NOPE:(h + 1) * QK_NOPE])
        pair = slice((h // 2) * LANES, (h // 2 + 1) * LANES)
        cos_t, sin_t = (cos_hi, sin_hi) if h % 2 else (cos_lo, sin_lo)
        cols.append(rope[:, pair] * cos_t + rope_sw[:, pair] * sin_t)
    return [jnp.concatenate(cols, axis=1)]


def _ep_kv_up(accs, e_refs):
    acc_k, acc_v = accs
    k_rope = e_refs[0][...].astype(F32)
    k_rope_hi = pltpu.roll(k_rope, LANES - QK_ROPE, 1)
    ones = jnp.ones((acc_v.shape[0], V_HEAD_STRIDE - V_DIM), F32)
    k_cols, v_cols = [], []
    for h in range(acc_k.shape[1] // QK_NOPE):
        k_cols += [acc_k[:, h * QK_NOPE:(h + 1) * QK_NOPE], k_rope_hi if h % 2 else k_rope]
        v_cols += [acc_v[:, h * V_DIM:(h + 1) * V_DIM], ones]
    return [jnp.concatenate(k_cols, axis=1), jnp.concatenate(v_cols, axis=1)]


def _rglru_kernel(ts, tc, gate_ref, rec_ref, cw_ref, cb_ref, wr_ref, br_ref, wi_ref, bi_ref,
                  lam_ref, ride32_ref, o_ref, ride16_ref, tail_ref, h_ref, a_scr, x_scr):
    t = pl.program_id(2)
    ride16_ref[...] = ride32_ref[...].astype(ride16_ref.dtype)

    @pl.when(t == 0)
    def _():
        tail_ref[...] = jnp.zeros_like(tail_ref)
        h_ref[...] = jnp.zeros_like(h_ref)

    x = rec_ref[0].astype(F32)
    tail = tail_ref[...]
    cw = cw_ref[...]
    row8 = lax.broadcasted_iota(jnp.int32, (SUBLANES, tc), 0)
    u = x * cw[CONV_WIDTH - 1:CONV_WIDTH, :] + cb_ref[...]
    for k in range(1, CONV_WIDTH):
        xs = pltpu.roll(x, k, 0)
        top = jnp.where(row8 < k, pltpu.roll(tail, k, 0), xs[0:SUBLANES])
        xs = jnp.concatenate([top, xs[SUBLANES:]], axis=0)
        u = u + xs * cw[CONV_WIDTH - 1 - k:CONV_WIDTH - k, :]
    tail_ref[...] = x[ts - SUBLANES:ts]

    ub = u.astype(BF16)
    r_parts, i_parts = [], []
    for g in range(tc // LRU_BLOCK):
        ug = ub[:, g * LRU_BLOCK:(g + 1) * LRU_BLOCK]
        r_parts.append(jnp.dot(ug, wr_ref[g], preferred_element_type=F32))
        i_parts.append(jnp.dot(ug, wi_ref[g], preferred_element_type=F32))
    r = jax.nn.sigmoid(jnp.concatenate(r_parts, axis=1) + br_ref[...])
    ig = jax.nn.sigmoid(jnp.concatenate(i_parts, axis=1) + bi_ref[...])

    neg_lam = -lam_ref[...]
    softplus = jnp.maximum(neg_lam, 0.0) + jnp.log1p(jnp.exp(-jnp.abs(neg_lam)))
    log_a = r * ((-LRU_C) * softplus)
    a = jnp.exp(log_a)
    z = -jnp.tanh(log_a) * (1.0 + a * a)
    xin = (ig * u) * jnp.where(z > 0.0, z * lax.rsqrt(z), 0.0)

    nb = ts // SUBLANES
    grow = lax.broadcasted_iota(jnp.int32, (nb, LANES), 0)
    gate = jax.nn.gelu(gate_ref[0].astype(F32), approximate=True)
    for c in range(tc // LANES):
        lanes = slice(c * LANES, (c + 1) * LANES)
        a_scr[c] = a[:, lanes]
        x_scr[c] = xin[:, lanes]
        decay, state = [], []
        for r in range(SUBLANES):
            a_r = a_scr[c, pl.ds(r, nb, stride=SUBLANES), :]
            x_r = x_scr[c, pl.ds(r, nb, stride=SUBLANES), :]
            decay.append(a_r if r == 0 else a_r * decay[-1])
            state.append(x_r if r == 0 else a_r * state[-1] + x_r)
        ca, cx = decay[-1], state[-1]
        shift = 1
        while shift < nb:
            if shift < SUBLANES:
                keep = grow >= shift
                a_s = jnp.where(keep, pltpu.roll(ca, shift, 0), 1.0)
                x_s = jnp.where(keep, pltpu.roll(cx, shift, 0), 0.0)
            else:
                a_s = jnp.concatenate([jnp.ones((shift, LANES), F32), ca[:nb - shift]], axis=0)
                x_s = jnp.concatenate([jnp.zeros((shift, LANES), F32), cx[:nb - shift]], axis=0)
            cx = ca * x_s + cx
            ca = ca * a_s
            shift *= 2
        h_prev = h_ref[0:1, lanes]
        group_out = ca * h_prev + cx
        group_in = jnp.where(grow == 0, h_prev, pltpu.roll(group_out, 1, 0))
        for r in range(SUBLANES):
            x_scr[c, pl.ds(r, nb, stride=SUBLANES), :] = decay[r] * group_in + state[r]
        h_ref[:, lanes] = jnp.broadcast_to(group_out[nb - 1:nb, :], (SUBLANES, LANES))
        o_ref[0, :, lanes] = (gate[:, lanes] * x_scr[c]).astype(o_ref.dtype)


def _rglru(gr, conv_w, conv_b, w_r, b_r, w_i, b_i, lam, batch, seq, rider32, *, ts=512, tc=512):
    d = conv_w.shape[1]
    ts, tc = min(ts, seq), min(tc, d)
    nc, nt = d // tc, seq // ts
    gpb = tc // LRU_BLOCK
    gr3 = gr.reshape(batch, seq, 2 * d)
    vec = pl.BlockSpec((1, tc), lambda b, c, t: (0, c))
    wspec = pl.BlockSpec((gpb, LRU_BLOCK, LRU_BLOCK), lambda b, c, t: (c, 0, 0))
    r_rows, r_cols = rider32.shape
    slab = _rider_slab(r_rows, batch * nc * nt)
    rider = pl.BlockSpec((slab, r_cols), functools.partial(
        lambda last, b, c, t: (jnp.minimum((b * nc + c) * nt + t, last), 0), r_rows // slab - 1))
    out, rider16 = pl.pallas_call(
        functools.partial(_rglru_kernel, ts, tc),
        out_shape=[jax.ShapeDtypeStruct((batch, seq, d), BF16),
                   jax.ShapeDtypeStruct((r_rows, r_cols), BF16)],
        grid=(batch, nc, nt),
        in_specs=[
            pl.BlockSpec((1, ts, tc), lambda b, c, t: (b, t, c)),
            pl.BlockSpec((1, ts, tc), lambda b, c, t: (b, t, nc + c)),
            pl.BlockSpec((CONV_WIDTH, tc), lambda b, c, t: (0, c)),
            vec, wspec, vec, wspec, vec, vec, rider,
        ],
        out_specs=[pl.BlockSpec((1, ts, tc), lambda b, c, t: (b, t, c)), rider],
        scratch_shapes=[pltpu.VMEM((SUBLANES, tc), F32), pltpu.VMEM((SUBLANES, tc), F32),
                        pltpu.VMEM((tc // LANES, ts, LANES), F32),
                        pltpu.VMEM((tc // LANES, ts, LANES), F32)],
        compiler_params=_cparams("arbitrary", "arbitrary", "arbitrary"),
        name="rglru",
    )(gr3, gr3, conv_w, conv_b.reshape(1, d), w_r.astype(BF16), b_r.reshape(1, d),
      w_i.astype(BF16), b_i.reshape(1, d), lam.reshape(1, d), rider32)
    return out.reshape(batch * seq, d), rider16


def _attn_kernel(tq, hps, q_ref, k_ref, v_ref, o_ref, sa_ref, sb_ref, m_ref, acc_ref):
    qi = pl.program_id(2)
    m_ref[...] = jnp.full_like(m_ref, NEG_BIG)
    acc_ref[...] = jnp.zeros_like(acc_ref)
    contract_last = (((1,), (1,)), ((), ()))

    def scores(j, s_ref):
        k0 = pl.multiple_of(j * tq, tq)
        for h in range(hps):
            q = q_ref[0, :, h * Q_HEAD_STRIDE:(h + 1) * Q_HEAD_STRIDE]
            k = k_ref[0, pl.ds(k0, tq), h * Q_HEAD_STRIDE:(h + 1) * Q_HEAD_STRIDE]
            s_ref[h] = lax.dot_general(q, k, contract_last, preferred_element_type=F32)

    def consume(j, s_ref, masked):
        k0 = pl.multiple_of(j * tq, tq)
        for h in range(hps):
            s = s_ref[h]
            if masked:
                qc = lax.broadcasted_iota(jnp.int32, s.shape, 0) // CHUNK
                kc = lax.broadcasted_iota(jnp.int32, s.shape, 1) // CHUNK
                s = jnp.where(qc >= kc, s, NEG_BIG)
            m_prev = m_ref[h]
            m_next = jnp.maximum(m_prev, jnp.max(s, axis=-1, keepdims=True))
            p = jnp.exp2(s - jnp.tile(m_next, (1, tq // LANES)))
            alpha = jnp.exp2(m_prev - m_next)
            m_ref[h] = m_next
            v = v_ref[0, pl.ds(k0, tq), h * V_HEAD_STRIDE:(h + 1) * V_HEAD_STRIDE]
            acc_ref[h] = (jnp.tile(alpha, (1, V_HEAD_STRIDE // LANES)) * acc_ref[h]
                          + jnp.dot(p.astype(BF16), v, preferred_element_type=F32))

    scores(0, sa_ref)

    def pair(t, carry):
        scores(2 * t + 1, sb_ref)
        consume(2 * t, sa_ref, False)
        scores(2 * t + 2, sa_ref)
        consume(2 * t + 1, sb_ref, False)
        return carry

    lax.fori_loop(0, qi // 2, pair, 0)

    @pl.when(qi % 2 == 1)
    def _():
        scores(qi, sb_ref)
        consume(qi - 1, sa_ref, False)
        consume(qi, sb_ref, True)

    @pl.when(qi % 2 == 0)
    def _():
        consume(qi, sa_ref, True)

    for h in range(hps):
        acc = acc_ref[h]
        o_ref[0, :, h * V_DIM:(h + 1) * V_DIM] = (acc[:, :V_DIM] / acc[:, V_DIM:]).astype(o_ref.dtype)


def _attention(q_cat, k_cat, v_ones, batch, seq, n_heads, *, tq=512, hps=4):
    tq = min(tq, seq)
    q3 = q_cat.reshape(batch, seq, n_heads * Q_HEAD_STRIDE)
    k3 = k_cat.reshape(batch, seq, n_heads * Q_HEAD_STRIDE)
    v3 = v_ones.reshape(batch, seq, n_heads * V_HEAD_STRIDE)
    out = pl.pallas_call(
        functools.partial(_attn_kernel, tq, hps),
        out_shape=jax.ShapeDtypeStruct((batch, seq, n_heads * V_DIM), BF16),
        grid=(batch, n_heads // hps, seq // tq),
        in_specs=[
            pl.BlockSpec((1, tq, hps * Q_HEAD_STRIDE), lambda b, h, i: (b, i, h)),
            pl.BlockSpec((1, seq, hps * Q_HEAD_STRIDE), lambda b, h, i: (b, 0, h)),
            pl.BlockSpec((1, seq, hps * V_HEAD_STRIDE), lambda b, h, i: (b, 0, h)),
        ],
        out_specs=pl.BlockSpec((1, tq, hps * V_DIM), lambda b, h, i: (b, i, h)),
        scratch_shapes=[pltpu.VMEM((hps, tq, tq), F32), pltpu.VMEM((hps, tq, tq), F32),
                        pltpu.VMEM((hps, tq, LANES), F32),
                        pltpu.VMEM((hps, tq, V_HEAD_STRIDE), F32)],
        compiler_params=_cparams("parallel", "parallel", "arbitrary"),
        name="attention",
    )(q3, k3, v3)
    return out.reshape(batch * seq, n_heads * V_DIM)


def _moe_router_kernel(x_ref, g_ref, wh_ref, wl_ref, xn_ref, idx_ref, gate_ref):
    x = x_ref[...]
    xn = _rms(x, g_ref[...])
    xn_ref[...] = xn
    xh = xn.astype(BF16)
    xl = (xn - xh.astype(F32)).astype(BF16)
    wh, wl = wh_ref[...], wl_ref[...]
    logits = (jnp.dot(xh, wh, preferred_element_type=F32) + jnp.dot(xh, wl, preferred_element_type=F32)
              + jnp.dot(xl, wh, preferred_element_type=F32))
    n_e = logits.shape[1]
    lane = lax.broadcasted_iota(jnp.int32, logits.shape, 1)
    m1 = jnp.max(logits, axis=-1, keepdims=True)
    i1 = jnp.min(jnp.where(logits == m1, lane, n_e), axis=-1, keepdims=True)
    rest = jnp.where(lane == i1, -jnp.inf, logits)
    m2 = jnp.max(rest, axis=-1, keepdims=True)
    i2 = jnp.min(jnp.where(rest == m2, lane, n_e), axis=-1, keepdims=True)
    e2 = jnp.exp(m2 - m1)
    denom = 1.0 + e2
    slot = lax.broadcasted_iota(jnp.int32, idx_ref.shape, 1)
    idx_ref[...] = jnp.where(slot == 0, i1, i2)
    gate_ref[...] = jnp.where(slot == 0, 1.0 / denom, e2 / denom)


def _moe_router(x, g, w_router, tm=256):
    m, d = x.shape
    tm = min(tm, m)
    n_e = w_router.shape[1]
    w_hi = w_router.astype(BF16)
    row = pl.BlockSpec((tm, d), lambda i: (i, 0))
    small = pl.BlockSpec((tm, TOP_K), lambda i: (i, 0))
    return pl.pallas_call(
        _moe_router_kernel,
        out_shape=[jax.ShapeDtypeStruct((m, d), F32),
                   jax.ShapeDtypeStruct((m, TOP_K), jnp.int32),
                   jax.ShapeDtypeStruct((m, TOP_K), F32)],
        grid=(m // tm,),
        in_specs=[row, pl.BlockSpec((1, d), lambda i: (0, 0)),
                  pl.BlockSpec((d, n_e), lambda i: (0, 0)),
                  pl.BlockSpec((d, n_e), lambda i: (0, 0))],
        out_specs=[row, small, small],
        compiler_params=_cparams("parallel"),
        name="moe_norm_router",
    )(x, g.reshape(1, d), w_hi, (w_router - w_hi.astype(F32)).astype(BF16))


def _row_copy(src_hbm, src_row, buf, dst_row, sem):
    return pltpu.make_async_copy(src_hbm.at[pl.ds(src_row, 1)], buf.at[pl.ds(dst_row, 1)], sem)


DMA_LOOP_UNROLL = 8


def _moe_gather_kernel(blk, tok_ref, nused_ref, x_hbm, o_ref, buf, sem):
    b = pl.program_id(0)
    n_used = nused_ref[0]

    def fetch(block, slot):
        def issue(r, c):
            _row_copy(x_hbm, tok_ref[block * blk + r], buf.at[slot], r, sem.at[slot]).start()
            return c

        lax.fori_loop(0, blk, issue, 0, unroll=DMA_LOOP_UNROLL)

    @pl.when((b == 0) & (n_used > 0))
    def _():
        fetch(0, 0)

    @pl.when(b + 1 < n_used)
    def _():
        fetch(b + 1, (b + 1) % 2)

    @pl.when(b < n_used)
    def _():
        slot = b % 2

        def drain(r, c):
            _row_copy(x_hbm, 0, buf.at[slot], r, sem.at[slot]).wait()
            return c

        lax.fori_loop(0, blk, drain, 0, unroll=DMA_LOOP_UNROLL)
        o_ref[...] = buf[slot].astype(o_ref.dtype)

    @pl.when(b >= n_used)
    def _():
        o_ref[...] = jnp.zeros_like(o_ref)


def _moe_gather(xn, row_tok, n_used, blk):
    n_rows = row_tok.shape[0]
    d = xn.shape[1]
    return pl.pallas_call(
        functools.partial(_moe_gather_kernel, blk),
        out_shape=jax.ShapeDtypeStruct((n_rows, d), BF16),
        grid_spec=pltpu.PrefetchScalarGridSpec(
            num_scalar_prefetch=2,
            grid=(n_rows // blk,),
            in_specs=[pl.BlockSpec(memory_space=pl.ANY)],
            out_specs=pl.BlockSpec((blk, d), lambda b, tok, nu: (b, 0)),
            scratch_shapes=[pltpu.VMEM((2, blk, d), F32), pltpu.SemaphoreType.DMA((2,))],
        ),
        compiler_params=_cparams("arbitrary"),
        name="moe_gather",
    )(row_tok, n_used, xn)


def _for_filled_rows(blk, n_valid, compute, o_ref):
    half = blk // 2

    @pl.when(n_valid > half)
    def _():
        o_ref[...] = compute(slice(0, blk)).astype(o_ref.dtype)

    @pl.when((n_valid > 0) & (n_valid <= half))
    def _():
        o_ref[0:half, :] = compute(slice(0, half)).astype(o_ref.dtype)
        o_ref[half:blk, :] = jnp.zeros((blk - half, o_ref.shape[1]), o_ref.dtype)

    @pl.when(n_valid == 0)
    def _():
        o_ref[...] = jnp.zeros_like(o_ref)


def _moe_up_kernel(blk, be_ref, valid_ref, x_ref, wg_ref, wu_ref, wd32_ref, o_ref, wd16_ref):
    wd16_ref[...] = wd32_ref[...].astype(wd16_ref.dtype)

    def compute(rows):
        x = x_ref[rows, :]
        g = jnp.dot(x, wg_ref[0], preferred_element_type=F32)
        u = jnp.dot(x, wu_ref[0], preferred_element_type=F32)
        return g * jax.nn.sigmoid(g) * u

    _for_filled_rows(blk, valid_ref[pl.program_id(0)], compute, o_ref)


def _moe_down_kernel(blk, be_ref, valid_ref, h_ref, wd_ref, o_ref):
    def compute(rows):
        return jnp.dot(h_ref[rows, :], wd_ref[0], preferred_element_type=F32)

    _for_filled_rows(blk, valid_ref[pl.program_id(0)], compute, o_ref)


def _moe_experts(x_rows, block_e, block_valid, w_gate, w_up, w_down_f32, blk, *, tf=1024, tn=1024):
    n_rows, d = x_rows.shape
    n_e, _, f = w_gate.shape
    tf, tn = min(tf, f), min(tn, d)
    nf, nn = f // tf, d // tn
    n_blocks = n_rows // blk

    def wcol(nj):
        return lambda b, j, be, nv: (be[b], 0, jnp.where(nv[b] > 0, j, nj - 1))

    wd2 = w_down_f32.reshape(n_e * f, d)
    slab = _rider_slab(n_e * f, n_blocks * nf)
    rider = pl.BlockSpec((slab, d), functools.partial(
        lambda last, b, j, be, nu: (jnp.minimum(b * nf + j, last), 0), n_e * f // slab - 1))
    h_rows, w_down = pl.pallas_call(
        functools.partial(_moe_up_kernel, blk),
        out_shape=[jax.ShapeDtypeStruct((n_rows, f), BF16), jax.ShapeDtypeStruct((n_e * f, d), BF16)],
        grid_spec=pltpu.PrefetchScalarGridSpec(
            num_scalar_prefetch=2,
            grid=(n_blocks, nf),
            in_specs=[pl.BlockSpec((blk, d), lambda b, j, be, nu: (b, 0)),
                      pl.BlockSpec((1, d, tf), wcol(nf)),
                      pl.BlockSpec((1, d, tf), wcol(nf)),
                      rider],
            out_specs=[pl.BlockSpec((blk, tf), lambda b, j, be, nu: (b, j)), rider],
        ),
        compiler_params=_cparams("arbitrary", "arbitrary"),
        name="moe_gate_up",
    )(block_e, block_valid, x_rows, w_gate, w_up, wd2)
    w_down = w_down.reshape(n_e, f, d)
    y_rows = pl.pallas_call(
        functools.partial(_moe_down_kernel, blk),
        out_shape=jax.ShapeDtypeStruct((n_rows, d), F32),
        grid_spec=pltpu.PrefetchScalarGridSpec(
            num_scalar_prefetch=2,
            grid=(n_rows // blk, nn),
            in_specs=[pl.BlockSpec((blk, f), lambda b, j, be, nu: (b, 0)),
                      pl.BlockSpec((1, f, tn), wcol(nn))],
            out_specs=pl.BlockSpec((blk, tn), lambda b, j, be, nu: (b, j)),
        ),
        compiler_params=_cparams("arbitrary", "arbitrary"),
        name="moe_down",
    )(block_e, block_valid, h_rows, w_down)
    return y_rows


def _moe_combine_kernel(tt, pos_ref, x_ref, gate_ref, g_ref, y_hbm, o_ref, buf, sem):
    i = pl.program_id(0)
    n_tiles = pl.num_programs(0)

    def fetch(tile, slot):
        def issue(r, c):
            for s in range(TOP_K):
                _row_copy(y_hbm, pos_ref[(tile * tt + r) * TOP_K + s], buf.at[slot, s], r,
                          sem.at[slot]).start()
            return c

        lax.fori_loop(0, tt, issue, 0, unroll=DMA_LOOP_UNROLL)

    @pl.when(i == 0)
    def _():
        fetch(0, 0)

    @pl.when(i + 1 < n_tiles)
    def _():
        fetch(i + 1, (i + 1) % 2)

    slot = i % 2

    def drain(r, c):
        for s in range(TOP_K):
            _row_copy(y_hbm, 0, buf.at[slot, s], r, sem.at[slot]).wait()
        return c

    lax.fori_loop(0, tt, drain, 0, unroll=DMA_LOOP_UNROLL)
    gates = gate_ref[...]
    y = buf[slot, 0] * gates[:, 0:1] + buf[slot, 1] * gates[:, 1:2]
    o_ref[...] = _rms(x_ref[...] + y, g_ref[...])


def _moe_combine(x, y_rows, pos, gates, g, tt=256):
    m, d = x.shape
    tt = min(tt, m)
    row = lambda i, p: (i, 0)
    return pl.pallas_call(
        functools.partial(_moe_combine_kernel, tt),
        out_shape=jax.ShapeDtypeStruct((m, d), F32),
        grid_spec=pltpu.PrefetchScalarGridSpec(
            num_scalar_prefetch=1,
            grid=(m // tt,),
            in_specs=[pl.BlockSpec((tt, d), row),
                      pl.BlockSpec((tt, TOP_K), row),
                      pl.BlockSpec((1, d), lambda i, p: (0, 0)),
                      pl.BlockSpec(memory_space=pl.ANY)],
            out_specs=pl.BlockSpec((tt, d), row),
            scratch_shapes=[pltpu.VMEM((2, TOP_K, tt, d), F32), pltpu.SemaphoreType.DMA((2,))],
        ),
        compiler_params=_cparams("arbitrary"),
        name="moe_combine_norm",
    )(pos, x, gates, g.reshape(1, d), y_rows)


def _moe_routing(idx, n_experts, blk):
    n = idx.shape[0]
    flat_e = idx.reshape(-1)
    onehot = (flat_e[:, None] == jnp.arange(n_experts, dtype=jnp.int32)[None, :]).astype(jnp.int32)
    csum = jnp.cumsum(onehot, axis=0)
    rank = jnp.sum(csum * onehot, axis=1) - 1
    counts = csum[-1]
    padded = (counts + blk - 1) // blk * blk
    pend = jnp.cumsum(padded)
    pstart = pend - padded
    dest = (pstart[flat_e] + rank).astype(jnp.int32)
    n_rows = -(-(n * TOP_K + n_experts * (blk - 1)) // blk) * blk
    n_blocks = n_rows // blk
    flat_tok = jnp.arange(n * TOP_K, dtype=jnp.int32) // TOP_K
    row_tok = jnp.zeros((n_rows,), jnp.int32).at[dest].set(flat_tok)
    block_e = jnp.minimum(
        jnp.searchsorted(pend, jnp.arange(n_blocks, dtype=jnp.int32) * blk, side='right'),
        n_experts - 1).astype(jnp.int32)
    n_used = (pend[-1] // blk).astype(jnp.int32).reshape(1)
    block_start = jnp.arange(n_blocks, dtype=jnp.int32) * blk
    block_valid = jnp.clip(pstart[block_e] + counts[block_e] - block_start, 0, blk)
    block_valid = jnp.where(block_start < pend[-1], block_valid, 0).astype(jnp.int32)
    return dest, row_tok, block_e, block_valid, n_used


def _rope_tables(seq):
    inv_freq = ROPE_THETA ** (-jnp.arange(0, QK_ROPE, 2, dtype=F32) / QK_ROPE)
    ang = jnp.arange(seq, dtype=F32)[:, None] * inv_freq[None, :]
    cos, sin = jnp.cos(ang), jnp.sin(ang)
    pad = jnp.zeros((seq, LANES - QK_ROPE), F32)
    return (jnp.concatenate([cos, cos, pad], axis=1),
            jnp.concatenate([-sin, sin, pad], axis=1))


def kernel(x, a_norm, a_w_in, a_conv_w, a_conv_b, a_w_rgate, a_b_rgate, a_w_igate, a_b_igate, a_lambda, a_w_out, kv_norm, kv_w_down, kv_latent_norm, kv_w_rope, kv_w_up_k, kv_w_up_v, b_norm, b_w_q_down, b_q_latent_norm, b_w_q_up, b_w_q_rope, b_w_out, ffn_norm, ffn_w_gate, ffn_w_up, ffn_w_down, moe_norm, moe_w_router, moe_w_gate, moe_w_up, moe_w_down, final_norm):
    batch, seq, d = x.shape
    n = batch * seq
    n_heads = kv_w_up_k.shape[1]
    assert a_norm.shape[0] == 1 and b_norm.shape[0] == 1, "depth-2 block: one RG-LRU and one MLA layer"
    xf = x.reshape(n, d)
    tm = min(1024, seq)
    n_seq_tiles = seq // tm

    def mm_residual(a, w, res, name, riders=(), stats=False, tn=512, tk=None):
        tn = min(tn, w.shape[1])
        outs = [(w.shape[1], F32, tn)] + ([(w.shape[1], BF16, tn), ("rowsum",)] if stats else [])
        return _matmul(a, [w], [tn], _ep_residual_stats if stats else _ep_residual,
                       [(res, (tm, tn), lambda i, j: (i, j))], outs,
                       tm=tm, tk=tk, riders=riders, name=name)

    def row_stat(ssq):
        return (ssq, (tm, LANES), lambda i, j: (i, 0))

    n_experts, _, f_exp = moe_w_gate.shape[1:]
    moe_gate32 = moe_w_gate[0].reshape(n_experts * d, f_exp)
    moe_up32 = moe_w_up[0].reshape(n_experts * d, f_exp)

    xn, = _norm(xf, [a_norm[0]])
    w_in = a_w_in[0].astype(BF16)
    tn = min(1024, w_in.shape[1])
    gr, w_out_a, wg, wu = _matmul(
        xn, [w_in], [tn], _ep_identity, [], [(w_in.shape[1], BF16, tn)], tm=tm,
        riders=[(a_w_out[0], None), (ffn_w_gate[0], ffn_norm[0]), (ffn_w_up[0], ffn_norm[0])],
        name="rglru_in")
    gated, moe_up16 = _rglru(gr, a_conv_w[0], a_conv_b[0], a_w_rgate[0], a_b_rgate[0], a_w_igate[0],
                             a_b_igate[0], a_lambda[0], batch, seq, moe_up32)
    x1, x1_16, ssq1, wd = mm_residual(gated, w_out_a, xf, "rglru_out", riders=[(ffn_w_down[0], None)],
                                      stats=True)

    f = wg.shape[1]
    tf = min(512, f)
    hid, moe_gate16 = _matmul(x1_16, [wg, wu], [tf, tf], functools.partial(_ep_swiglu_normed, d),
                              [row_stat(ssq1)], [(f, BF16, tf)], tm=tm,
                              riders=[(moe_gate32, None)], name="ffn_gate_up")
    tk_down = next(t for t in (3072, 2048, 1024, 512, 256, f) if f % t == 0)
    x2, x2_16, ssq2 = mm_residual(hid, wd, x1, "ffn_down", stats=True, tn=1024, tk=tk_down)

    cos_t, sin_t = _rope_tables(seq)
    rope_specs = [(cos_t, (tm, LANES), lambda i, j: (i % n_seq_tiles, 0)),
                  (sin_t, (tm, LANES), lambda i, j: (i % n_seq_tiles, 0))]
    kv_rank = kv_w_down.shape[1]
    w_kv_down = (kv_w_down * kv_norm[:, None]).astype(BF16)
    w_kv_rope = jnp.pad(kv_w_rope * kv_norm[:, None], ((0, 0), (0, LANES - QK_ROPE))).astype(BF16)
    c_kv, k_rope = _matmul(
        x2_16, [w_kv_down, w_kv_rope], [kv_rank, LANES], functools.partial(_ep_kv_down, d),
        [row_stat(ssq2), (kv_latent_norm.reshape(1, kv_rank), (1, kv_rank), lambda i, j: (0, 0))]
        + rope_specs,
        [(kv_rank, BF16, kv_rank), (LANES, BF16, LANES)], tm=tm, name="kv_down")
    hk = n_heads * QK_NOPE
    tn = min(1024, hk)
    k_cat, v = _matmul(
        c_kv, [kv_w_up_k.reshape(kv_rank, hk).astype(BF16),
               kv_w_up_v.reshape(kv_rank, n_heads * V_DIM).astype(BF16)],
        [tn, tn], _ep_kv_up, [(k_rope, (tm, LANES), lambda i, j: (i, 0))],
        [(n_heads * Q_HEAD_STRIDE, BF16, tn // QK_NOPE * Q_HEAD_STRIDE),
         (n_heads * V_HEAD_STRIDE, BF16, tn // V_DIM * V_HEAD_STRIDE)],
        tm=tm, name="kv_up")

    q_rank = b_w_q_down.shape[2]
    c_q, = _matmul(x2_16, [(b_w_q_down[0] * b_norm[0][:, None]).astype(BF16)], [q_rank],
                   functools.partial(_ep_rms_normed, d),
                   [row_stat(ssq2),
                    (b_q_latent_norm[0].reshape(1, q_rank), (1, q_rank), lambda i, j: (0, 0))],
                   [(q_rank, BF16, q_rank)], tm=tm, name="q_down")
    hq = min(8, n_heads)
    q_scale = ATTN_SCALE * LOG2_E
    q_tables = [cos_t * q_scale, sin_t * q_scale]
    q_tables += [jnp.roll(t, LANES - QK_ROPE, axis=1) for t in q_tables]
    q_rope_specs = [(t, (tm, LANES), lambda i, j: (i % n_seq_tiles, 0)) for t in q_tables]
    w_q_rope = b_w_q_rope[0]
    w_q_rope_sw = jnp.roll(w_q_rope, QK_ROPE // 2, axis=-1)
    q_cat, = _matmul(c_q, [(b_w_q_up[0] * q_scale).reshape(q_rank, n_heads * QK_NOPE).astype(BF16),
                           w_q_rope.reshape(q_rank, n_heads * QK_ROPE).astype(BF16),
                           w_q_rope_sw.reshape(q_rank, n_heads * QK_ROPE).astype(BF16)],
                     [hq * QK_NOPE, hq * QK_ROPE, hq * QK_ROPE], _ep_q_up, q_rope_specs,
                     [(n_heads * Q_HEAD_STRIDE, BF16, hq * Q_HEAD_STRIDE)], tm=tm, name="q_up")
    o = _attention(q_cat, k_cat, v, batch, seq, n_heads)
    x3, = mm_residual(o, b_w_out[0].astype(BF16), x2, "attn_out")

    xn_moe, top_idx, gates = _moe_router(x3, moe_norm[0], moe_w_router[0])
    dest, row_tok, block_e, block_valid, n_used = _moe_routing(top_idx, n_experts, MOE_BLOCK)
    x_rows = _moe_gather(xn_moe, row_tok, n_used, MOE_BLOCK)
    y_rows = _moe_experts(x_rows, block_e, block_valid, moe_gate16.reshape(n_experts, d, f_exp),
                          moe_up16.reshape(n_experts, d, f_exp), moe_w_down[0], MOE_BLOCK)
    out = _moe_combine(x3, y_rows, dest, gates, final_norm)
    return out.reshape(batch, seq, d)
```

```python
import functools

import jax
import jax.numpy as jnp
from jax import lax
from jax.experimental import pallas as pl
from jax.experimental.pallas import tpu as pltpu

F32 = jnp.float32
BF16 = jnp.bfloat16

CHUNK = 64
LRU_BLOCK = 256
CONV_WIDTH = 4
LRU_C = 8.0
QK_NOPE = 128
QK_ROPE = 64
V_DIM = 128
ROPE_THETA = 10000.0
ATTN_SCALE = (QK_NOPE + QK_ROPE) ** -0.5
TOP_K = 2
EPS = 1e-6

VMEM_LIMIT_BYTES = 56 * 1024 * 1024
LANES = 128
SUBLANES = 8
BF16_SUBLANES = 16
Q_HEAD_STRIDE = 2 * LANES
V_HEAD_STRIDE = 2 * V_DIM
NEG_BIG = -1e30
LOG2_E = 1.4426950408889634

MOE_BLOCK = 512


def _cparams(*sem):
    return pltpu.CompilerParams(dimension_semantics=sem, vmem_limit_bytes=VMEM_LIMIT_BYTES)


def _rms(x, g):
    return x * lax.rsqrt(jnp.mean(x * x, axis=-1, keepdims=True) + EPS) * g


def _norm_kernel(n_out, x_ref, *refs):
    g_refs, o_refs = refs[:n_out], refs[n_out:]
    x = x_ref[...]
    y = x * lax.rsqrt(jnp.mean(x * x, axis=-1, keepdims=True) + EPS)
    for g_ref, o_ref in zip(g_refs, o_refs):
        o_ref[...] = (y * g_ref[...]).astype(o_ref.dtype)


def _norm(x, gains, out_dtype=BF16, tm=256):
    m, d = x.shape
    tm = min(tm, m)
    n = len(gains)
    row = pl.BlockSpec((tm, d), lambda i: (i, 0))
    vec = pl.BlockSpec((1, d), lambda i: (0, 0))
    outs = pl.pallas_call(
        functools.partial(_norm_kernel, n),
        out_shape=[jax.ShapeDtypeStruct((m, d), out_dtype)] * n,
        grid=(m // tm,),
        in_specs=[row] + [vec] * n,
        out_specs=[row] * n,
        compiler_params=_cparams("parallel"),
        name="rmsnorm",
    )(x, *[g.reshape(1, d) for g in gains])
    return outs


def _mm_kernel(nw, ne, rider_gain, out_kinds, nk, epilogue, a_ref, *refs):
    nr, ng, no = len(rider_gain), sum(rider_gain), len(out_kinds)
    refs = list(refs)
    take = lambda n: [refs.pop(0) for _ in range(n)]
    w_refs, e_refs, rin_refs, gain_refs = take(nw), take(ne), take(nr), take(ng)
    o_refs, rout_refs, acc_refs = take(no), take(nr), refs

    gains = iter(gain_refs)
    for rin, rout, has_gain in zip(rin_refs, rout_refs, rider_gain):
        w = rin[...] * next(gains)[...] if has_gain else rin[...]
        rout[...] = w.astype(rout.dtype)

    def finish(accs):
        vals = epilogue(accs, e_refs)
        for o_ref, v, kind in zip(o_refs, vals, out_kinds):
            if kind == "tile":
                o_ref[...] = v.astype(o_ref.dtype)
                continue
            j = pl.program_id(1)
            v = jnp.broadcast_to(v, o_ref.shape)

            @pl.when(j == 0)
            def _():
                o_ref[...] = v

            @pl.when(j > 0)
            def _():
                o_ref[...] += v

    if nk == 1:
        finish([jnp.dot(a_ref[...], w[...], preferred_element_type=F32) for w in w_refs])
        return

    k = pl.program_id(2)

    @pl.when(k == 0)
    def _():
        for acc in acc_refs:
            acc[...] = jnp.zeros_like(acc)

    for acc, w in zip(acc_refs, w_refs):
        acc[...] += jnp.dot(a_ref[...], w[...], preferred_element_type=F32)

    @pl.when(k == nk - 1)
    def _():
        finish([acc[...] for acc in acc_refs])


def _rider_slab(rows, n_steps):
    slab = -(-rows // n_steps)
    slab = -(-slab // BF16_SUBLANES) * BF16_SUBLANES
    while rows % slab:
        slab += BF16_SUBLANES
    return slab


def _matmul(a, ws, tns, epilogue, extras, outs, *, tm, tk=None, riders=(), name):
    m, kdim = a.shape
    tm = min(tm, m)
    tk = kdim if tk is None else min(tk, kdim)
    assert m % tm == 0 and kdim % tk == 0, (m, tm, kdim, tk)
    nk = kdim // tk
    nj = ws[0].shape[1] // tns[0]
    n_steps = (m // tm) * nj * nk
    in_specs = [pl.BlockSpec((tm, tk), lambda i, j, k: (i, k))]
    for w, tn in zip(ws, tns):
        assert w.shape[1] // tn == nj
        in_specs.append(pl.BlockSpec((tk, tn), lambda i, j, k: (k, j)))
    for _, bs, im in extras:
        in_specs.append(pl.BlockSpec(bs, functools.partial(lambda im, i, j, k: im(i, j), im)))
    out_shape, out_specs, out_kinds = [], [], []
    for o in outs:
        if o[0] == "rowsum":
            out_shape.append(jax.ShapeDtypeStruct((m, LANES), F32))
            out_specs.append(pl.BlockSpec((tm, LANES), lambda i, j, k: (i, 0)))
            out_kinds.append("rowsum")
        else:
            n, dt, tn = o
            out_shape.append(jax.ShapeDtypeStruct((m, n), dt))
            out_specs.append(pl.BlockSpec((tm, tn), lambda i, j, k: (i, j)))
            out_kinds.append("tile")
    rider_specs, gain_specs, gain_args = [], [], []
    for r, gain in riders:
        rows, cols = r.shape
        slab = _rider_slab(rows, n_steps)
        slab_map = functools.partial(
            lambda last, i, j, k: (jnp.minimum((i * nj + j) * nk + k, last), 0), rows // slab - 1)
        rider_specs.append(pl.BlockSpec((slab, cols), slab_map))
        out_shape.append(jax.ShapeDtypeStruct((rows, cols), BF16))
        if gain is not None:
            gain_specs.append(pl.BlockSpec((slab, 1), slab_map))
            gain_args.append(gain.reshape(rows, 1))
    scratch = [pltpu.VMEM((tm, tn), F32) for tn in tns] if nk > 1 else []
    sequential = bool(riders) or "rowsum" in out_kinds
    semantics = ("arbitrary",) * 3 if sequential else ("parallel", "parallel", "arbitrary")
    res = pl.pallas_call(
        functools.partial(_mm_kernel, len(ws), len(extras), tuple(g is not None for _, g in riders),
                          tuple(out_kinds), nk, epilogue),
        out_shape=out_shape,
        grid=(m // tm, nj, nk),
        in_specs=in_specs + rider_specs + gain_specs,
        out_specs=out_specs + rider_specs,
        scratch_shapes=scratch,
        compiler_params=_cparams(*semantics),
        name=name,
    )(a, *ws, *[e[0] for e in extras], *[r for r, _ in riders], *gain_args)
    return res


def _ep_identity(accs, e_refs):
    return accs


def _ep_residual(accs, e_refs):
    return [e_refs[0][...] + accs[0]]


def _ep_residual_stats(accs, e_refs):
    x_new = e_refs[0][...] + accs[0]
    return [x_new, x_new, jnp.sum(x_new * x_new, axis=-1, keepdims=True)]


def _inv_rms(ssq_ref, d):
    return lax.rsqrt(ssq_ref[...][:, 0:1] * (1.0 / d) + EPS)


def _ep_swiglu_normed(d, accs, e_refs):
    r = _inv_rms(e_refs[0], d)
    g, u = accs[0] * r, accs[1] * r
    return [g * jax.nn.sigmoid(g) * u]


def _ep_rms_normed(d, accs, e_refs):
    ssq_ref, g_ref = e_refs
    return [_rms(accs[0] * _inv_rms(ssq_ref, d), g_ref[...])]


def _rope_slab(x, cos_t, sin_t):
    lane = lax.broadcasted_iota(jnp.int32, x.shape, 1)
    swapped = jnp.where(lane < QK_ROPE // 2,
                        pltpu.roll(x, LANES - QK_ROPE // 2, 1),
                        pltpu.roll(x, QK_ROPE // 2, 1))
    return x * cos_t + swapped * sin_t


def _ep_kv_down(d, accs, e_refs):
    ssq_ref, g_ref, cos_ref, sin_ref = e_refs
    r = _inv_rms(ssq_ref, d)
    return [_rms(accs[0] * r, g_ref[...]), _rope_slab(accs[1] * r, cos_ref[...], sin_ref[...])]


def _ep_q_up(accs, e_refs):
    cos_lo, sin_lo, cos_hi, sin_hi = (r[...] for r in e_refs)
    nope, rope, rope_sw = accs
    cols = []
    for h in range(nope.shape[1] // QK_NOPE):
        cols.append(nope[:, h * QK_NOPE:(h + 1) * QK_NOPE])
        pair = slice((h // 2) * LANES, (h // 2 + 1) * LANES)
        cos_t, sin_t = (cos_hi, sin_hi) if h % 2 else (cos_lo, sin_lo)
        cols.append(rope[:, pair] * cos_t + rope_sw[:, pair] * sin_t)
    return [jnp.concatenate(cols, axis=1)]


def _ep_kv_up(accs, e_refs):
    acc_k, acc_v = accs
    k_rope = e_refs[0][...].astype(F32)
    k_rope_hi = pltpu.roll(k_rope, LANES - QK_ROPE, 1)
    ones = jnp.ones((acc_v.shape[0], V_HEAD_STRIDE - V_DIM), F32)
    k_cols, v_cols = [], []
    for h in range(acc_k.shape[1] // QK_NOPE):
        k_cols += [acc_k[:, h * QK_NOPE:(h + 1) * QK_NOPE], k_rope_hi if h % 2 else k_rope]
        v_cols += [acc_v[:, h * V_DIM:(h + 1) * V_DIM], ones]
    return [jnp.concatenate(k_cols, axis=1), jnp.concatenate(v_cols, axis=1)]


def _rglru_kernel(ts, tc, gate_ref, rec_ref, cw_ref, cb_ref, wr_ref, br_ref, wi_ref, bi_ref,
                  lam_ref, ride32_ref, o_ref, ride16_ref, tail_ref, h_ref, a_scr, x_scr):
    t = pl.program_id(2)
    ride16_ref[...] = ride32_ref[...].astype(ride16_ref.dtype)

    @pl.when(t == 0)
    def _():
        tail_ref[...] = jnp.zeros_like(tail_ref)
        h_ref[...] = jnp.zeros_like(h_ref)

    x = rec_ref[0].astype(F32)
    tail = tail_ref[...]
    cw = cw_ref[...]
    row8 = lax.broadcasted_iota(jnp.int32, (SUBLANES, tc), 0)
    u = x * cw[CONV_WIDTH - 1:CONV_WIDTH, :] + cb_ref[...]
    for k in range(1, CONV_WIDTH):
        xs = pltpu.roll(x, k, 0)
        top = jnp.where(row8 < k, pltpu.roll(tail, k, 0), xs[0:SUBLANES])
        xs = jnp.concatenate([top, xs[SUBLANES:]], axis=0)
        u = u + xs * cw[CONV_WIDTH - 1 - k:CONV_WIDTH - k, :]
    tail_ref[...] = x[ts - SUBLANES:ts]

    ub = u.astype(BF16)
    r_parts, i_parts = [], []
    for g in range(tc // LRU_BLOCK):
        ug = ub[:, g * LRU_BLOCK:(g + 1) * LRU_BLOCK]
        r_parts.append(jnp.dot(ug, wr_ref[g], preferred_element_type=F32))
        i_parts.append(jnp.dot(ug, wi_ref[g], preferred_element_type=F32))
    r = jax.nn.sigmoid(jnp.concatenate(r_parts, axis=1) + br_ref[...])
    ig = jax.nn.sigmoid(jnp.concatenate(i_parts, axis=1) + bi_ref[...])

    neg_lam = -lam_ref[...]
    softplus = jnp.maximum(neg_lam, 0.0) + jnp.log1p(jnp.exp(-jnp.abs(neg_lam)))
    log_a = r * ((-LRU_C) * softplus)
    a = jnp.exp(log_a)
    z = -jnp.tanh(log_a) * (1.0 + a * a)
    xin = (ig * u) * jnp.where(z > 0.0, z * lax.rsqrt(z), 0.0)

    nb = ts // SUBLANES
    grow = lax.broadcasted_iota(jnp.int32, (nb, LANES), 0)
    gate = jax.nn.gelu(gate_ref[0].astype(F32), approximate=True)
    for c in range(tc // LANES):
        lanes = slice(c * LANES, (c + 1) * LANES)
        a_scr[c] = a[:, lanes]
        x_scr[c] = xin[:, lanes]
        decay, state = [], []
        for r in range(SUBLANES):
            a_r = a_scr[c, pl.ds(r, nb, stride=SUBLANES), :]
            x_r = x_scr[c, pl.ds(r, nb, stride=SUBLANES), :]
            decay.append(a_r if r == 0 else a_r * decay[-1])
            state.append(x_r if r == 0 else a_r * state[-1] + x_r)
        ca, cx = decay[-1], state[-1]
        shift = 1
        while shift < nb:
            if shift < SUBLANES:
                keep = grow >= shift
                a_s = jnp.where(keep, pltpu.roll(ca, shift, 0), 1.0)
                x_s = jnp.where(keep, pltpu.roll(cx, shift, 0), 0.0)
            else:
                a_s = jnp.concatenate([jnp.ones((shift, LANES), F32), ca[:nb - shift]], axis=0)
                x_s = jnp.concatenate([jnp.zeros((shift, LANES), F32), cx[:nb - shift]], axis=0)
            cx = ca * x_s + cx
            ca = ca * a_s
            shift *= 2
        h_prev = h_ref[0:1, lanes]
        group_out = ca * h_prev + cx
        group_in = jnp.where(grow == 0, h_prev, pltpu.roll(group_out, 1, 0))
        for r in range(SUBLANES):
            x_scr[c, pl.ds(r, nb, stride=SUBLANES), :] = decay[r] * group_in + state[r]
        h_ref[:, lanes] = jnp.broadcast_to(group_out[nb - 1:nb, :], (SUBLANES, LANES))
        o_ref[0, :, lanes] = (gate[:, lanes] * x_scr[c]).astype(o_ref.dtype)


def _rglru(gr, conv_w, conv_b, w_r, b_r, w_i, b_i, lam, batch, seq, rider32, *, ts=512, tc=1024):
    d = conv_w.shape[1]
    ts, tc = min(ts, seq), min(tc, d)
    nc, nt = d // tc, seq // ts
    gpb = tc // LRU_BLOCK
    gr3 = gr.reshape(batch, seq, 2 * d)
    vec = pl.BlockSpec((1, tc), lambda b, c, t: (0, c))
    wspec = pl.BlockSpec((gpb, LRU_BLOCK, LRU_BLOCK), lambda b, c, t: (c, 0, 0))
    r_rows, r_cols = rider32.shape
    slab = _rider_slab(r_rows, batch * nc * nt)
    rider = pl.BlockSpec((slab, r_cols), functools.partial(
        lambda last, b, c, t: (jnp.minimum((b * nc + c) * nt + t, last), 0), r_rows // slab - 1))
    out, rider16 = pl.pallas_call(
        functools.partial(_rglru_kernel, ts, tc),
        out_shape=[jax.ShapeDtypeStruct((batch, seq, d), BF16),
                   jax.ShapeDtypeStruct((r_rows, r_cols), BF16)],
        grid=(batch, nc, nt),
        in_specs=[
            pl.BlockSpec((1, ts, tc), lambda b, c, t: (b, t, c)),
            pl.BlockSpec((1, ts, tc), lambda b, c, t: (b, t, nc + c)),
            pl.BlockSpec((CONV_WIDTH, tc), lambda b, c, t: (0, c)),
            vec, wspec, vec, wspec, vec, vec, rider,
        ],
        out_specs=[pl.BlockSpec((1, ts, tc), lambda b, c, t: (b, t, c)), rider],
        scratch_shapes=[pltpu.VMEM((SUBLANES, tc), F32), pltpu.VMEM((SUBLANES, tc), F32),
                        pltpu.VMEM((tc // LANES, ts, LANES), F32),
                        pltpu.VMEM((tc // LANES, ts, LANES), F32)],
        compiler_params=_cparams("arbitrary", "arbitrary", "arbitrary"),
        name="rglru",
    )(gr3, gr3, conv_w, conv_b.reshape(1, d), w_r.astype(BF16), b_r.reshape(1, d),
      w_i.astype(BF16), b_i.reshape(1, d), lam.reshape(1, d), rider32)
    return out.reshape(batch * seq, d), rider16


def _attn_kernel(tq, hps, q_ref, k_ref, v_ref, o_ref, sa_ref, sb_ref, m_ref, acc_ref):
    qi = pl.program_id(2)
    m_ref[...] = jnp.full_like(m_ref, NEG_BIG)
    acc_ref[...] = jnp.zeros_like(acc_ref)
    contract_last = (((1,), (1,)), ((), ()))

    def scores(j, s_ref):
        k0 = pl.multiple_of(j * tq, tq)
        for h in range(hps):
            q = q_ref[0, :, h * Q_HEAD_STRIDE:(h + 1) * Q_HEAD_STRIDE]
            k = k_ref[0, pl.ds(k0, tq), h * Q_HEAD_STRIDE:(h + 1) * Q_HEAD_STRIDE]
            s_ref[h] = lax.dot_general(q, k, contract_last, preferred_element_type=F32)

    def consume(j, s_ref, masked):
        k0 = pl.multiple_of(j * tq, tq)
        for h in range(hps):
            s = s_ref[h]
            if masked:
                qc = lax.broadcasted_iota(jnp.int32, s.shape, 0) // CHUNK
                kc = lax.broadcasted_iota(jnp.int32, s.shape, 1) // CHUNK
                s = jnp.where(qc >= kc, s, NEG_BIG)
            m_prev = m_ref[h]
            m_next = jnp.maximum(m_prev, jnp.max(s, axis=-1, keepdims=True))
            p = jnp.exp2(s - jnp.tile(m_next, (1, tq // LANES)))
            alpha = jnp.exp2(m_prev - m_next)
            m_ref[h] = m_next
            v = v_ref[0, pl.ds(k0, tq), h * V_HEAD_STRIDE:(h + 1) * V_HEAD_STRIDE]
            acc_ref[h] = (jnp.tile(alpha, (1, V_HEAD_STRIDE // LANES)) * acc_ref[h]
                          + jnp.dot(p.astype(BF16), v, preferred_element_type=F32))

    scores(0, sa_ref)

    def pair(t, carry):
        scores(2 * t + 1, sb_ref)
        consume(2 * t, sa_ref, False)
        scores(2 * t + 2, sa_ref)
        consume(2 * t + 1, sb_ref, False)
        return carry

    lax.fori_loop(0, qi // 2, pair, 0)

    @pl.when(qi % 2 == 1)
    def _():
        scores(qi, sb_ref)
        consume(qi - 1, sa_ref, False)
        consume(qi, sb_ref, True)

    @pl.when(qi % 2 == 0)
    def _():
        consume(qi, sa_ref, True)

    for h in range(hps):
        acc = acc_ref[h]
        o_ref[0, :, h * V_DIM:(h + 1) * V_DIM] = (acc[:, :V_DIM] / acc[:, V_DIM:]).astype(o_ref.dtype)


def _attention(q_cat, k_cat, v_ones, batch, seq, n_heads, *, tq=512, hps=4):
    tq = min(tq, seq)
    q3 = q_cat.reshape(batch, seq, n_heads * Q_HEAD_STRIDE)
    k3 = k_cat.reshape(batch, seq, n_heads * Q_HEAD_STRIDE)
    v3 = v_ones.reshape(batch, seq, n_heads * V_HEAD_STRIDE)
    out = pl.pallas_call(
        functools.partial(_attn_kernel, tq, hps),
        out_shape=jax.ShapeDtypeStruct((batch, seq, n_heads * V_DIM), BF16),
        grid=(batch, n_heads // hps, seq // tq),
        in_specs=[
            pl.BlockSpec((1, tq, hps * Q_HEAD_STRIDE), lambda b, h, i: (b, i, h)),
            pl.BlockSpec((1, seq, hps * Q_HEAD_STRIDE), lambda b, h, i: (b, 0, h)),
            pl.BlockSpec((1, seq, hps * V_HEAD_STRIDE), lambda b, h, i: (b, 0, h)),
        ],
        out_specs=pl.BlockSpec((1, tq, hps * V_DIM), lambda b, h, i: (b, i, h)),
        scratch_shapes=[pltpu.VMEM((hps, tq, tq), F32), pltpu.VMEM((hps, tq, tq), F32),
                        pltpu.VMEM((hps, tq, LANES), F32),
                        pltpu.VMEM((hps, tq, V_HEAD_STRIDE), F32)],
        compiler_params=_cparams("parallel", "parallel", "arbitrary"),
        name="attention",
    )(q3, k3, v3)
    return out.reshape(batch * seq, n_heads * V_DIM)


def _moe_router_kernel(x_ref, g_ref, wh_ref, wl_ref, xn_ref, idx_ref, gate_ref):
    x = x_ref[...]
    xn = _rms(x, g_ref[...])
    xn_ref[...] = xn
    xh = xn.astype(BF16)
    xl = (xn - xh.astype(F32)).astype(BF16)
    wh, wl = wh_ref[...], wl_ref[...]
    logits = (jnp.dot(xh, wh, preferred_element_type=F32) + jnp.dot(xh, wl, preferred_element_type=F32)
              + jnp.dot(xl, wh, preferred_element_type=F32))
    n_e = logits.shape[1]
    lane = lax.broadcasted_iota(jnp.int32, logits.shape, 1)
    m1 = jnp.max(logits, axis=-1, keepdims=True)
    i1 = jnp.min(jnp.where(logits == m1, lane, n_e), axis=-1, keepdims=True)
    rest = jnp.where(lane == i1, -jnp.inf, logits)
    m2 = jnp.max(rest, axis=-1, keepdims=True)
    i2 = jnp.min(jnp.where(rest == m2, lane, n_e), axis=-1, keepdims=True)
    e2 = jnp.exp(m2 - m1)
    denom = 1.0 + e2
    slot = lax.broadcasted_iota(jnp.int32, idx_ref.shape, 1)
    idx_ref[...] = jnp.where(slot == 0, i1, i2)
    gate_ref[...] = jnp.where(slot == 0, 1.0 / denom, e2 / denom)


def _moe_router(x, g, w_router, tm=256):
    m, d = x.shape
    tm = min(tm, m)
    n_e = w_router.shape[1]
    w_hi = w_router.astype(BF16)
    row = pl.BlockSpec((tm, d), lambda i: (i, 0))
    small = pl.BlockSpec((tm, TOP_K), lambda i: (i, 0))
    return pl.pallas_call(
        _moe_router_kernel,
        out_shape=[jax.ShapeDtypeStruct((m, d), F32),
                   jax.ShapeDtypeStruct((m, TOP_K), jnp.int32),
                   jax.ShapeDtypeStruct((m, TOP_K), F32)],
        grid=(m // tm,),
        in_specs=[row, pl.BlockSpec((1, d), lambda i: (0, 0)),
                  pl.BlockSpec((d, n_e), lambda i: (0, 0)),
                  pl.BlockSpec((d, n_e), lambda i: (0, 0))],
        out_specs=[row, small, small],
        compiler_params=_cparams("parallel"),
        name="moe_norm_router",
    )(x, g.reshape(1, d), w_hi, (w_router - w_hi.astype(F32)).astype(BF16))


def _row_copy(src_hbm, src_row, buf, dst_row, sem):
    return pltpu.make_async_copy(src_hbm.at[pl.ds(src_row, 1)], buf.at[pl.ds(dst_row, 1)], sem)


DMA_LOOP_UNROLL = 8


def _moe_gather_kernel(blk, tok_ref, nused_ref, x_hbm, o_ref, buf, sem):
    b = pl.program_id(0)
    n_used = nused_ref[0]

    def fetch(block, slot):
        def issue(r, c):
            _row_copy(x_hbm, tok_ref[block * blk + r], buf.at[slot], r, sem.at[slot]).start()
            return c

        lax.fori_loop(0, blk, issue, 0, unroll=DMA_LOOP_UNROLL)

    @pl.when((b == 0) & (n_used > 0))
    def _():
        fetch(0, 0)

    @pl.when(b + 1 < n_used)
    def _():
        fetch(b + 1, (b + 1) % 2)

    @pl.when(b < n_used)
    def _():
        slot = b % 2

        def drain(r, c):
            _row_copy(x_hbm, 0, buf.at[slot], r, sem.at[slot]).wait()
            return c

        lax.fori_loop(0, blk, drain, 0, unroll=DMA_LOOP_UNROLL)
        o_ref[...] = buf[slot].astype(o_ref.dtype)

    @pl.when(b >= n_used)
    def _():
        o_ref[...] = jnp.zeros_like(o_ref)


def _moe_gather(xn, row_tok, n_used, blk):
    n_rows = row_tok.shape[0]
    d = xn.shape[1]
    return pl.pallas_call(
        functools.partial(_moe_gather_kernel, blk),
        out_shape=jax.ShapeDtypeStruct((n_rows, d), BF16),
        grid_spec=pltpu.PrefetchScalarGridSpec(
            num_scalar_prefetch=2,
            grid=(n_rows // blk,),
            in_specs=[pl.BlockSpec(memory_space=pl.ANY)],
            out_specs=pl.BlockSpec((blk, d), lambda b, tok, nu: (b, 0)),
            scratch_shapes=[pltpu.VMEM((2, blk, d), F32), pltpu.SemaphoreType.DMA((2,))],
        ),
        compiler_params=_cparams("arbitrary"),
        name="moe_gather",
    )(row_tok, n_used, xn)


def _for_filled_rows(blk, n_valid, compute, o_ref):
    half = blk // 2

    @pl.when(n_valid > half)
    def _():
        o_ref[...] = compute(slice(0, blk)).astype(o_ref.dtype)

    @pl.when((n_valid > 0) & (n_valid <= half))
    def _():
        o_ref[0:half, :] = compute(slice(0, half)).astype(o_ref.dtype)
        o_ref[half:blk, :] = jnp.zeros((blk - half, o_ref.shape[1]), o_ref.dtype)

    @pl.when(n_valid == 0)
    def _():
        o_ref[...] = jnp.zeros_like(o_ref)


def _moe_up_kernel(blk, be_ref, valid_ref, x_ref, wg_ref, wu_ref, wd32_ref, o_ref, wd16_ref):
    wd16_ref[...] = wd32_ref[...].astype(wd16_ref.dtype)

    def compute(rows):
        x = x_ref[rows, :]
        g = jnp.dot(x, wg_ref[0], preferred_element_type=F32)
        u = jnp.dot(x, wu_ref[0], preferred_element_type=F32)
        return g * jax.nn.sigmoid(g) * u

    _for_filled_rows(blk, valid_ref[pl.program_id(0)], compute, o_ref)


def _moe_down_kernel(blk, be_ref, valid_ref, h_ref, wd_ref, o_ref):
    def compute(rows):
        return jnp.dot(h_ref[rows, :], wd_ref[0], preferred_element_type=F32)

    _for_filled_rows(blk, valid_ref[pl.program_id(0)], compute, o_ref)


def _moe_experts(x_rows, block_e, block_valid, w_gate, w_up, w_down_f32, blk, *, tf=1024, tn=1024):
    n_rows, d = x_rows.shape
    n_e, _, f = w_gate.shape
    tf, tn = min(tf, f), min(tn, d)
    nf, nn = f // tf, d // tn
    n_blocks = n_rows // blk

    def wcol(nj):
        return lambda b, j, be, nv: (be[b], 0, jnp.where(nv[b] > 0, j, nj - 1))

    wd2 = w_down_f32.reshape(n_e * f, d)
    slab = _rider_slab(n_e * f, n_blocks * nf)
    rider = pl.BlockSpec((slab, d), functools.partial(
        lambda last, b, j, be, nu: (jnp.minimum(b * nf + j, last), 0), n_e * f // slab - 1))
    h_rows, w_down = pl.pallas_call(
        functools.partial(_moe_up_kernel, blk),
        out_shape=[jax.ShapeDtypeStruct((n_rows, f), BF16), jax.ShapeDtypeStruct((n_e * f, d), BF16)],
        grid_spec=pltpu.PrefetchScalarGridSpec(
            num_scalar_prefetch=2,
            grid=(n_blocks, nf),
            in_specs=[pl.BlockSpec((blk, d), lambda b, j, be, nu: (b, 0)),
                      pl.BlockSpec((1, d, tf), wcol(nf)),
                      pl.BlockSpec((1, d, tf), wcol(nf)),
                      rider],
            out_specs=[pl.BlockSpec((blk, tf), lambda b, j, be, nu: (b, j)), rider],
        ),
        compiler_params=_cparams("arbitrary", "arbitrary"),
        name="moe_gate_up",
    )(block_e, block_valid, x_rows, w_gate, w_up, wd2)
    w_down = w_down.reshape(n_e, f, d)
    y_rows = pl.pallas_call(
        functools.partial(_moe_down_kernel, blk),
        out_shape=jax.ShapeDtypeStruct((n_rows, d), F32),
        grid_spec=pltpu.PrefetchScalarGridSpec(
            num_scalar_prefetch=2,
            grid=(n_rows // blk, nn),
            in_specs=[pl.BlockSpec((blk, f), lambda b, j, be, nu: (b, 0)),
                      pl.BlockSpec((1, f, tn), wcol(nn))],
            out_specs=pl.BlockSpec((blk, tn), lambda b, j, be, nu: (b, j)),
        ),
        compiler_params=_cparams("arbitrary", "arbitrary"),
        name="moe_down",
    )(block_e, block_valid, h_rows, w_down)
    return y_rows


def _moe_combine_kernel(tt, pos_ref, x_ref, gate_ref, g_ref, y_hbm, o_ref, buf, sem):
    i = pl.program_id(0)
    n_tiles = pl.num_programs(0)

    def fetch(tile, slot):
        def issue(r, c):
            for s in range(TOP_K):
                _row_copy(y_hbm, pos_ref[(tile * tt + r) * TOP_K + s], buf.at[slot, s], r,
                          sem.at[slot]).start()
            return c

        lax.fori_loop(0, tt, issue, 0, unroll=DMA_LOOP_UNROLL)

    @pl.when(i == 0)
    def _():
        fetch(0, 0)

    @pl.when(i + 1 < n_tiles)
    def _():
        fetch(i + 1, (i + 1) % 2)

    slot = i % 2

    def drain(r, c):
        for s in range(TOP_K):
            _row_copy(y_hbm, 0, buf.at[slot, s], r, sem.at[slot]).wait()
        return c

    lax.fori_loop(0, tt, drain, 0, unroll=DMA_LOOP_UNROLL)
    gates = gate_ref[...]
    y = buf[slot, 0] * gates[:, 0:1] + buf[slot, 1] * gates[:, 1:2]
    o_ref[...] = _rms(x_ref[...] + y, g_ref[...])


def _moe_combine(x, y_rows, pos, gates, g, tt=256):
    m, d = x.shape
    tt = min(tt, m)
    row = lambda i, p: (i, 0)
    return pl.pallas_call(
        functools.partial(_moe_combine_kernel, tt),
        out_shape=jax.ShapeDtypeStruct((m, d), F32),
        grid_spec=pltpu.PrefetchScalarGridSpec(
            num_scalar_prefetch=1,
            grid=(m // tt,),
            in_specs=[pl.BlockSpec((tt, d), row),
                      pl.BlockSpec((tt, TOP_K), row),
                      pl.BlockSpec((1, d), lambda i, p: (0, 0)),
                      pl.BlockSpec(memory_space=pl.ANY)],
            out_specs=pl.BlockSpec((tt, d), row),
            scratch_shapes=[pltpu.VMEM((2, TOP_K, tt, d), F32), pltpu.SemaphoreType.DMA((2,))],
        ),
        compiler_params=_cparams("arbitrary"),
        name="moe_combine_norm",
    )(pos, x, gates, g.reshape(1, d), y_rows)


def _moe_routing(idx, n_experts, blk):
    n = idx.shape[0]
    flat_e = idx.reshape(-1)
    onehot = (flat_e[:, None] == jnp.arange(n_experts, dtype=jnp.int32)[None, :]).astype(jnp.int32)
    csum = jnp.cumsum(onehot, axis=0)
    rank = jnp.sum(csum * onehot, axis=1) - 1
    counts = csum[-1]
    padded = (counts + blk - 1) // blk * blk
    pend = jnp.cumsum(padded)
    pstart = pend - padded
    dest = (pstart[flat_e] + rank).astype(jnp.int32)
    n_rows = -(-(n * TOP_K + n_experts * (blk - 1)) // blk) * blk
    n_blocks = n_rows // blk
    flat_tok = jnp.arange(n * TOP_K, dtype=jnp.int32) // TOP_K
    row_tok = jnp.zeros((n_rows,), jnp.int32).at[dest].set(flat_tok)
    block_e = jnp.minimum(
        jnp.searchsorted(pend, jnp.arange(n_blocks, dtype=jnp.int32) * blk, side='right'),
        n_experts - 1).astype(jnp.int32)
    n_used = (pend[-1] // blk).astype(jnp.int32).reshape(1)
    block_start = jnp.arange(n_blocks, dtype=jnp.int32) * blk
    block_valid = jnp.clip(pstart[block_e] + counts[block_e] - block_start, 0, blk)
    block_valid = jnp.where(block_start < pend[-1], block_valid, 0).astype(jnp.int32)
    return dest, row_tok, block_e, block_valid, n_used


def _rope_tables(seq):
    inv_freq = ROPE_THETA ** (-jnp.arange(0, QK_ROPE, 2, dtype=F32) / QK_ROPE)
    ang = jnp.arange(seq, dtype=F32)[:, None] * inv_freq[None, :]
    cos, sin = jnp.cos(ang), jnp.sin(ang)
    pad = jnp.zeros((seq, LANES - QK_ROPE), F32)
    return (jnp.concatenate([cos, cos, pad], axis=1),
            jnp.concatenate([-sin, sin, pad], axis=1))


def kernel(x, a_norm, a_w_in, a_conv_w, a_conv_b, a_w_rgate, a_b_rgate, a_w_igate, a_b_igate, a_lambda, a_w_out, kv_norm, kv_w_down, kv_latent_norm, kv_w_rope, kv_w_up_k, kv_w_up_v, b_norm, b_w_q_down, b_q_latent_norm, b_w_q_up, b_w_q_rope, b_w_out, ffn_norm, ffn_w_gate, ffn_w_up, ffn_w_down, moe_norm, moe_w_router, moe_w_gate, moe_w_up, moe_w_down, final_norm):
    batch, seq, d = x.shape
    n = batch * seq
    n_heads = kv_w_up_k.shape[1]
    assert a_norm.shape[0] == 1 and b_norm.shape[0] == 1, "depth-2 block: one RG-LRU and one MLA layer"
    xf = x.reshape(n, d)
    tm = min(1024, seq)
    n_seq_tiles = seq // tm

    def mm_residual(a, w, res, name, riders=(), stats=False, tn=512, tk=None):
        tn = min(tn, w.shape[1])
        outs = [(w.shape[1], F32, tn)] + ([(w.shape[1], BF16, tn), ("rowsum",)] if stats else [])
        return _matmul(a, [w], [tn], _ep_residual_stats if stats else _ep_residual,
                       [(res, (tm, tn), lambda i, j: (i, j))], outs,
                       tm=tm, tk=tk, riders=riders, name=name)

    def row_stat(ssq):
        return (ssq, (tm, LANES), lambda i, j: (i, 0))

    n_experts, _, f_exp = moe_w_gate.shape[1:]
    moe_gate32 = moe_w_gate[0].reshape(n_experts * d, f_exp)
    moe_up32 = moe_w_up[0].reshape(n_experts * d, f_exp)

    xn, = _norm(xf, [a_norm[0]])
    w_in = a_w_in[0].astype(BF16)
    tn = min(1024, w_in.shape[1])
    gr, w_out_a, wg, wu = _matmul(
        xn, [w_in], [tn], _ep_identity, [], [(w_in.shape[1], BF16, tn)], tm=tm,
        riders=[(a_w_out[0], None), (ffn_w_gate[0], ffn_norm[0]), (ffn_w_up[0], ffn_norm[0])],
        name="rglru_in")
    gated, moe_up16 = _rglru(gr, a_conv_w[0], a_conv_b[0], a_w_rgate[0], a_b_rgate[0], a_w_igate[0],
                             a_b_igate[0], a_lambda[0], batch, seq, moe_up32)
    x1, x1_16, ssq1, wd = mm_residual(gated, w_out_a, xf, "rglru_out", riders=[(ffn_w_down[0], None)],
                                      stats=True)

    f = wg.shape[1]
    tf = min(512, f)
    hid, moe_gate16 = _matmul(x1_16, [wg, wu], [tf, tf], functools.partial(_ep_swiglu_normed, d),
                              [row_stat(ssq1)], [(f, BF16, tf)], tm=tm,
                              riders=[(moe_gate32, None)], name="ffn_gate_up")
    tk_down = next(t for t in (3072, 2048, 1024, 512, 256, f) if f % t == 0)
    x2, x2_16, ssq2 = mm_residual(hid, wd, x1, "ffn_down", stats=True, tn=1024, tk=tk_down)

    cos_t, sin_t = _rope_tables(seq)
    rope_specs = [(cos_t, (tm, LANES), lambda i, j: (i % n_seq_tiles, 0)),
                  (sin_t, (tm, LANES), lambda i, j: (i % n_seq_tiles, 0))]
    kv_rank = kv_w_down.shape[1]
    w_kv_down = (kv_w_down * kv_norm[:, None]).astype(BF16)
    w_kv_rope = jnp.pad(kv_w_rope * kv_norm[:, None], ((0, 0), (0, LANES - QK_ROPE))).astype(BF16)
    c_kv, k_rope = _matmul(
        x2_16, [w_kv_down, w_kv_rope], [kv_rank, LANES], functools.partial(_ep_kv_down, d),
        [row_stat(ssq2), (kv_latent_norm.reshape(1, kv_rank), (1, kv_rank), lambda i, j: (0, 0))]
        + rope_specs,
        [(kv_rank, BF16, kv_rank), (LANES, BF16, LANES)], tm=tm, name="kv_down")
    hk = n_heads * QK_NOPE
    tn = min(2048, hk)
    k_cat, v = _matmul(
        c_kv, [kv_w_up_k.reshape(kv_rank, hk).astype(BF16),
               kv_w_up_v.reshape(kv_rank, n_heads * V_DIM).astype(BF16)],
        [tn, tn], _ep_kv_up, [(k_rope, (tm, LANES), lambda i, j: (i, 0))],
        [(n_heads * Q_HEAD_STRIDE, BF16, tn // QK_NOPE * Q_HEAD_STRIDE),
         (n_heads * V_HEAD_STRIDE, BF16, tn // V_DIM * V_HEAD_STRIDE)],
        tm=tm, name="kv_up")

    q_rank = b_w_q_down.shape[2]
    c_q, = _matmul(x2_16, [(b_w_q_down[0] * b_norm[0][:, None]).astype(BF16)], [q_rank],
                   functools.partial(_ep_rms_normed, d),
                   [row_stat(ssq2),
                    (b_q_latent_norm[0].reshape(1, q_rank), (1, q_rank), lambda i, j: (0, 0))],
                   [(q_rank, BF16, q_rank)], tm=tm, name="q_down")
    hq = min(8, n_heads)
    q_scale = ATTN_SCALE * LOG2_E
    q_tables = [cos_t * q_scale, sin_t * q_scale]
    q_tables += [jnp.roll(t, LANES - QK_ROPE, axis=1) for t in q_tables]
    q_rope_specs = [(t, (tm, LANES), lambda i, j: (i % n_seq_tiles, 0)) for t in q_tables]
    w_q_rope = b_w_q_rope[0]
    w_q_rope_sw = jnp.roll(w_q_rope, QK_ROPE // 2, axis=-1)
    q_cat, = _matmul(c_q, [(b_w_q_up[0] * q_scale).reshape(q_rank, n_heads * QK_NOPE).astype(BF16),
                           w_q_rope.reshape(q_rank, n_heads * QK_ROPE).astype(BF16),
                           w_q_rope_sw.reshape(q_rank, n_heads * QK_ROPE).astype(BF16)],
                     [hq * QK_NOPE, hq * QK_ROPE, hq * QK_ROPE], _ep_q_up, q_rope_specs,
                     [(n_heads * Q_HEAD_STRIDE, BF16, hq * Q_HEAD_STRIDE)], tm=tm, name="q_up")
    o = _attention(q_cat, k_cat, v, batch, seq, n_heads)
    x3, = mm_residual(o, b_w_out[0].astype(BF16), x2, "attn_out", tn=1024)

    xn_moe, top_idx, gates = _moe_router(x3, moe_norm[0], moe_w_router[0])
    dest, row_tok, block_e, block_valid, n_used = _moe_routing(top_idx, n_experts, MOE_BLOCK)
    x_rows = _moe_gather(xn_moe, row_tok, n_used, MOE_BLOCK)
    y_rows = _moe_experts(x_rows, block_e, block_valid, moe_gate16.reshape(n_experts, d, f_exp),
                          moe_up16.reshape(n_experts, d, f_exp), moe_w_down[0], MOE_BLOCK)
    out = _moe_combine(x3, y_rows, dest, gates, final_norm)
    return out.reshape(batch, seq, d)
```
